```python
import math
import jax, jax.numpy as jnp
from jax import lax
import numpy as np

D_MODEL = 1024
BATCH = 4
SEQ = 4096
DEPTH = 4

GRID_W = 64
CTX_LEN = 256
N_MIXERS = 3
N_HEADS = 16
HEAD_DIM = D_MODEL // N_HEADS
WIN_H = 8
WIN_W = 16
CONV_W = 3
POOL_WINDOWS = (2, 4, 8, 16)
N_POOL_GROUPS = len(POOL_WINDOWS)
POOL_GC = D_MODEL // N_POOL_GROUPS
D_FF = ((8 * D_MODEL // 3 + 255) // 256) * 256
N_MOD = 6
EPS = 1e-6
N_LAYERS_A = (DEPTH + 2) // 3
N_LAYERS_B = (DEPTH + 1) // 3
N_LAYERS_C = DEPTH // 3

kernel_name = "hybrid_natten_shortconv_pool_dit"


def rms_norm(x, g):
    xf = x.astype(jnp.float32)
    y = xf * lax.rsqrt(jnp.mean(xf * xf, axis=-1, keepdims=True) + EPS)
    return (y * g.astype(jnp.float32)).astype(x.dtype)


def modulate(h, shift, scale):
    return h * (1 + scale) + shift


def swiglu(h, w1, w3, w2):
    return (jax.nn.silu(h @ w1) * (h @ w3)) @ w2


def neighbourhood_attention(hx, hc, w_qkv, w_o, rpb, need_ctx_out):
    B, N, D = hx.shape
    L = hc.shape[1]
    rows = N // GRID_W
    kh = min(WIN_H, rows)
    scale = HEAD_DIM ** -0.5

    qkv = (hx @ w_qkv).reshape(B, rows, GRID_W, 3, N_HEADS, HEAD_DIM)
    q = qkv[:, :, :, 0] * scale
    k = qkv[:, :, :, 1]
    v = qkv[:, :, :, 2]
    qkv_c = (hc @ w_qkv).reshape(B, L, 3, N_HEADS, HEAD_DIM)
    q_c = qkv_c[:, :, 0] * scale
    k_c = qkv_c[:, :, 1]
    v_c = qkv_c[:, :, 2]

    qc = np.arange(GRID_W)[:, None]
    kc = np.arange(GRID_W)[None, :]
    col_start = np.clip(qc - WIN_W // 2, 0, GRID_W - WIN_W)
    col_mask = (kc >= col_start) & (kc < col_start + WIN_W)
    dc_idx = np.clip(kc - qc, -(WIN_W - 1), WIN_W - 1) + (WIN_W - 1)
    rpb_cols = rpb.astype(jnp.float32)[:, :, dc_idx]

    def row_block(r):
        start = jnp.clip(r - kh // 2, 0, rows - kh)
        k_band = lax.dynamic_slice_in_dim(k, start, kh, axis=1)
        v_band = lax.dynamic_slice_in_dim(v, start, kh, axis=1)
        q_r = lax.dynamic_index_in_dim(q, r, axis=1, keepdims=False)
        dr_idx = start + jnp.arange(kh) - r + (WIN_H - 1)
        bias = jnp.take(rpb_cols, dr_idx, axis=1).transpose(0, 2, 1, 3)
        s_lat = jnp.einsum('bqhd,brkhd->bhqrk', q_r, k_band).astype(jnp.float32) + bias[None]
        s_lat = jnp.where(col_mask[None, None, :, None, :], s_lat, -jnp.inf)
        s_lat = s_lat.reshape(B, N_HEADS, GRID_W, kh * GRID_W)
        s_ctx = jnp.einsum('bqhd,blhd->bhql', q_r, k_c).astype(jnp.float32)
        p = jax.nn.softmax(jnp.concatenate([s_lat, s_ctx], axis=-1), axis=-1).astype(v.dtype)
        o = jnp.einsum('bhqk,bkhd->bqhd', p[..., :kh * GRID_W],
                       v_band.reshape(B, kh * GRID_W, N_HEADS, HEAD_DIM))
        o = o + jnp.einsum('bhql,blhd->bqhd', p[..., kh * GRID_W:], v_c)
        return o

    o = lax.map(row_block, jnp.arange(rows))
    yx = o.transpose(1, 0, 2, 3, 4).reshape(B, N, D) @ w_o

    yc = None
    if need_ctx_out:
        s = jnp.einsum('blhd,bmhd->bhlm', q_c, k_c).astype(jnp.float32)
        p = jax.nn.softmax(s, axis=-1).astype(v_c.dtype)
        yc = jnp.einsum('bhlm,bmhd->blhd', p, v_c).reshape(B, L, D) @ w_o
    return yx, yc


def dwconv3_centred(u, w):
    up = jnp.pad(u, ((0, 0), (1, 1), (0, 0)))
    return up[:, :-2] * w[0] + up[:, 1:-1] * w[1] + up[:, 2:] * w[2]


def short_conv_mixer(h, w_in, conv_w, w_out):
    gb, gc, val = jnp.split(h @ w_in, 3, axis=-1)
    return (gb * dwconv3_centred(gc * val, conv_w)) @ w_out


def multiscale_pool_mixer(h, w_grp, ch_scale):
    B, N, D = h.shape
    hf = h.astype(jnp.float32)
    cs = jnp.concatenate([jnp.zeros((B, 1, D), jnp.float32), jnp.cumsum(hf, axis=1)], axis=1)
    t = np.arange(N)
    outs = []
    for g, win in enumerate(POOL_WINDOWS):
        lo = np.maximum(t - win // 2, 0)
        hi = np.minimum(t + win // 2, N)
        cnt = (hi - lo).astype(np.float32)
        sl = slice(g * POOL_GC, (g + 1) * POOL_GC)
        mean = (cs[:, hi, sl] - cs[:, lo, sl]) / cnt[None, :, None]
        outs.append(mean - hf[..., sl])
    pooled = jnp.stack(outs, axis=2).astype(h.dtype)
    y = jnp.einsum('bngc,gcd->bngd', pooled, w_grp).reshape(B, N, D)
    return y * ch_scale


def setup_inputs(seed: int = 0) -> dict:
    key = jax.random.key(seed)
    ks = jax.random.split(key, 20)
    D, F = D_MODEL, D_FF
    f32 = jnp.float32

    def nrm(k, shape, s):
        return jax.random.normal(k, shape, f32) * s

    return {
        "x": nrm(ks[0], (BATCH, SEQ, D), 1.0),
        "c": nrm(ks[1], (BATCH, D), 1.0),
        "ctx": nrm(ks[2], (BATCH, CTX_LEN, D), 1.0),
        "c_ctx": nrm(ks[3], (D,), 1.0),
        "adaln_w": nrm(ks[4], (DEPTH, D, N_MOD * D), 0.5 * D ** -0.5),
        "adaln_b": nrm(ks[5], (DEPTH, N_MOD * D), 0.02),
        "norm_mix_g": 1.0 + nrm(ks[6], (DEPTH, D), 0.05),
        "norm_ffn_g": 1.0 + nrm(ks[7], (DEPTH, D), 0.05),
        "ffn_w1": nrm(ks[8], (DEPTH, D, F), D ** -0.5),
        "ffn_w3": nrm(ks[9], (DEPTH, D, F), D ** -0.5),
        "ffn_w2": nrm(ks[10], (DEPTH, F, D), F ** -0.5),
        "na_w_qkv": nrm(ks[11], (N_LAYERS_A, D, 3 * D), D ** -0.5),
        "na_w_o": nrm(ks[12], (N_LAYERS_A, D, D), D ** -0.5),
        "na_rpb": nrm(ks[13], (N_LAYERS_A, N_HEADS, 2 * WIN_H - 1, 2 * WIN_W - 1), 0.2),
        "sc_w_in": nrm(ks[14], (N_LAYERS_B, D, 3 * D), D ** -0.5),
        "sc_conv_w": nrm(ks[15], (N_LAYERS_B, CONV_W, D), CONV_W ** -0.5),
        "sc_w_out": nrm(ks[16], (N_LAYERS_B, D, D), D ** -0.5),
        "pool_w": nrm(ks[17], (N_LAYERS_C, N_POOL_GROUPS, POOL_GC, POOL_GC), POOL_GC ** -0.5),
        "pool_scale": 1.0 + nrm(ks[18], (N_LAYERS_C, D), 0.1),
        "final_g": 1.0 + nrm(ks[19], (D,), 0.05),
    }


def reference(x, c, ctx, c_ctx, adaln_w, adaln_b, norm_mix_g, norm_ffn_g, ffn_w1, ffn_w3, ffn_w2,
              na_w_qkv, na_w_o, na_rpb, sc_w_in, sc_conv_w, sc_w_out, pool_w, pool_scale, final_g):
    silu_c = jax.nn.silu(c)
    silu_cc = jax.nn.silu(c_ctx)
    for layer in range(DEPTH):
        kind = layer % N_MIXERS
        j = layer // N_MIXERS
        need_ctx_out = layer < DEPTH - 1

        mx = jnp.split((silu_c @ adaln_w[layer] + adaln_b[layer])[:, None, :], N_MOD, axis=-1)
        mc = jnp.split(silu_cc @ adaln_w[layer] + adaln_b[layer], N_MOD, axis=-1)

        hx = modulate(rms_norm(x, norm_mix_g[layer]), mx[0], mx[1])
        if kind == 0 or need_ctx_out:
            hc = modulate(rms_norm(ctx, norm_mix_g[layer]), mc[0], mc[1])

        if kind == 0:
            yx, yc = neighbourhood_attention(hx, hc, na_w_qkv[j], na_w_o[j], na_rpb[j], need_ctx_out)
        elif kind == 1:
            yx = short_conv_mixer(hx, sc_w_in[j], sc_conv_w[j], sc_w_out[j])
            yc = short_conv_mixer(hc, sc_w_in[j], sc_conv_w[j], sc_w_out[j]) if need_ctx_out else None
        else:
            yx = multiscale_pool_mixer(hx, pool_w[j], pool_scale[j])
            yc = multiscale_pool_mixer(hc, pool_w[j], pool_scale[j]) if need_ctx_out else None

        x = x + mx[2] * yx
        hx = modulate(rms_norm(x, norm_ffn_g[layer]), mx[3], mx[4])
        x = x + mx[5] * swiglu(hx, ffn_w1[layer], ffn_w3[layer], ffn_w2[layer])

        if need_ctx_out:
            ctx = ctx + mc[2] * yc
            hc = modulate(rms_norm(ctx, norm_ffn_g[layer]), mc[3], mc[4])
            ctx = ctx + mc[5] * swiglu(hc, ffn_w1[layer], ffn_w3[layer], ffn_w2[layer])
    return rms_norm(x, final_g)
```

```python
import functools

import numpy as np
import jax
import jax.numpy as jnp
from jax import lax
from jax.experimental import pallas as pl
from jax.experimental.pallas import tpu as pltpu

F32 = jnp.float32
BF16 = jnp.bfloat16

GRID_W = 64
N_MIXERS = 3
N_HEADS = 16
WIN_H = 8
WIN_W = 16
CONV_W = 3
POOL_WINDOWS = (2, 4, 8, 16)
N_MOD = 6
EPS = 1e-6

V7X_VMEM_BYTES = 64 * 1024 * 1024
LANES = 128
F32_SUBLANES = 8
BF16_SUBLANES = 16

HEADS_PER_BLOCK = 2
MOD_ROWS = 8


def _vmem_limit(est_bytes):
    return int(min(V7X_VMEM_BYTES - 6 * 1024 * 1024, max(32 * 1024 * 1024, est_bytes * 5 // 4)))


def _params(est_bytes, ndims=1):
    return pltpu.CompilerParams(dimension_semantics=("arbitrary",) * ndims,
                                vmem_limit_bytes=_vmem_limit(est_bytes))


def _const_spec(shape):
    nd = len(shape)
    return pl.BlockSpec(shape, lambda *_: (0,) * nd, pipeline_mode=pl.Buffered(1))


def _split_bf16(v):
    hi = v.astype(BF16)
    lo = (v - hi.astype(F32)).astype(BF16)
    return hi, lo


def _mod_kernel(c_ref, w_ref, b_ref, o_ref):
    c = c_ref[...]
    s = c * jax.nn.sigmoid(c)
    s_hi, s_lo = _split_bf16(s)
    lhs = jnp.concatenate([s_hi, s_lo], axis=0)
    w_hi, w_lo = _split_bf16(w_ref[...])
    r = (jnp.dot(lhs, w_hi, preferred_element_type=F32)
         + jnp.dot(lhs, w_lo, preferred_element_type=F32))
    o_ref[...] = r[:MOD_ROWS] + r[MOD_ROWS:] + b_ref[...]


def _modulation(c_rows, adaln_w, adaln_b):
    depth, d, nm = adaln_w.shape
    nc = nm // 4
    est = 2 * d * nc * 4 + 3 * d * nc * 2 + 4 * MOD_ROWS * nc * 4
    return pl.pallas_call(
        _mod_kernel,
        grid=(depth, nm // nc),
        in_specs=[
            pl.BlockSpec((MOD_ROWS, d), lambda l, j: (0, 0)),
            pl.BlockSpec((None, d, nc), lambda l, j: (l, 0, j)),
            pl.BlockSpec((None, 1, nc), lambda l, j: (l, 0, j)),
        ],
        out_specs=pl.BlockSpec((None, MOD_ROWS, nc), lambda l, j: (l, 0, j)),
        out_shape=jax.ShapeDtypeStruct((depth, MOD_ROWS, nm), F32),
        compiler_params=_params(est, 2),
        name="adaln_mod",
    )(c_rows, adaln_w, adaln_b.reshape(depth, 1, nm))


def _mod_row(mod_ref, tm, tokens_per_group, group_base):
    grp = group_base + lax.div(pl.program_id(0) * tm, tokens_per_group)
    return mod_ref[pl.ds(grp, 1), :]


def _chunk(m, idx, d):
    return m[:, idx * d:(idx + 1) * d]


def _rms(x, g):
    ms = jnp.mean(x * x, axis=-1, keepdims=True)
    return x * lax.rsqrt(ms + EPS) * g


def _norm_mod(x, g, shift, scale):
    return _rms(x, g) * (1.0 + scale) + shift


def _ffn_tail(x1, m, g_ffn, w1_ref, w3_ref, w2_ref, hid_ref, fc):
    d = x1.shape[-1]
    f = w1_ref.shape[-1]
    h = _norm_mod(x1, g_ffn, _chunk(m, 3, d), _chunk(m, 4, d)).astype(BF16)
    for c in range(f // fc):
        sl = slice(c * fc, (c + 1) * fc)
        a = jnp.dot(h, w1_ref[:, sl], preferred_element_type=F32)
        b = jnp.dot(h, w3_ref[:, sl], preferred_element_type=F32)
        hid_ref[:, sl] = (a * jax.nn.sigmoid(a) * b).astype(BF16)
    y = jnp.dot(hid_ref[...], w2_ref[...], preferred_element_type=F32)
    return x1 + _chunk(m, 5, d) * y


def _finish(out, final_g_ref, o_ref):
    if final_g_ref is not None:
        out = _rms(out, final_g_ref[...])
    o_ref[...] = out


def _qkv_kernel(x_ref, mod_ref, g_ref, w_ref, o_ref, *, tm, tpg, gbase, q_scale):
    d = x_ref.shape[-1]
    m = _mod_row(mod_ref, tm, tpg, gbase)
    h = _norm_mod(x_ref[...], g_ref[...], _chunk(m, 0, d), _chunk(m, 1, d)).astype(BF16)
    for j in range(3):
        r = jnp.dot(h, w_ref[:, j * d:(j + 1) * d], preferred_element_type=F32)
        if j == 0:
            r = r * q_scale
        o_ref[:, j * d:(j + 1) * d] = r.astype(BF16)


def _conv_in_kernel(x_ref, mod_ref, g_ref, w_ref, gb_ref, z_ref, *, tm, tpg, gbase):
    d = x_ref.shape[-1]
    m = _mod_row(mod_ref, tm, tpg, gbase)
    h = _norm_mod(x_ref[...], g_ref[...], _chunk(m, 0, d), _chunk(m, 1, d)).astype(BF16)
    gb_ref[...] = jnp.dot(h, w_ref[:, :d], preferred_element_type=F32).astype(BF16)
    gc = jnp.dot(h, w_ref[:, d:2 * d], preferred_element_type=F32)
    val = jnp.dot(h, w_ref[:, 2 * d:], preferred_element_type=F32)
    z_ref[...] = (gc * val).astype(BF16)


def _front_call(kernel, x, mod, g, w, out_shapes, out_specs, tm, name):
    t, d = x.shape
    est = 2 * tm * d * 4 + w.size * 2 + 2 * tm * w.shape[1] * 2 + tm * w.shape[1] * 4 + 4 * tm * d * 4
    return pl.pallas_call(
        kernel,
        grid=(t // tm,),
        in_specs=[
            pl.BlockSpec((tm, d), lambda i: (i, 0)),
            _const_spec(mod.shape),
            _const_spec(g.shape),
            _const_spec(w.shape),
        ],
        out_specs=out_specs,
        out_shape=out_shapes,
        compiler_params=_params(est),
        name=name,
    )(x, mod, g, w)


def _qkv_proj(x, mod, g, w, tm, tpg, gbase, q_scale):
    t, d = x.shape
    kern = functools.partial(_qkv_kernel, tm=tm, tpg=tpg, gbase=gbase, q_scale=q_scale)
    return _front_call(kern, x, mod, g, w,
                       jax.ShapeDtypeStruct((t, 3 * d), BF16),
                       pl.BlockSpec((tm, 3 * d), lambda i: (i, 0)), tm, "qkv_proj")


def _conv_in_proj(x, mod, g, w, tm, tpg, gbase):
    t, d = x.shape
    kern = functools.partial(_conv_in_kernel, tm=tm, tpg=tpg, gbase=gbase)
    spec = pl.BlockSpec((tm, d), lambda i: (i, 0))
    return _front_call(kern, x, mod, g, w,
                       (jax.ShapeDtypeStruct((t, d), BF16), jax.ShapeDtypeStruct((t, d), BF16)),
                       (spec, spec), tm, "conv_in_proj")


def _stack_heads(q):
    lane = lax.broadcasted_iota(jnp.int32, q.shape, 1)
    zero = jnp.zeros_like(q)
    return jnp.concatenate([jnp.where(lane < LANES // 2, q, zero),
                            jnp.where(lane >= LANES // 2, q, zero)], axis=0)


def _merge_heads(o):
    mrows = o.shape[0] // 2
    lane = lax.broadcasted_iota(jnp.int32, (mrows, o.shape[1]), 1)
    return jnp.where(lane < LANES // 2, o[:mrows], o[mrows:])


_NT = (((1,), (1,)), ((), ()))


def _attn_kernel(q_ref, k_ref, v_ref, qc_ref, kc_ref, vc_ref, bias_ref, *out_refs, rows, kh, need_ctx):
    o_ref = out_refs[0]
    kc = kc_ref[...]
    vc = vc_ref[...]
    band = kh * GRID_W

    def row_body(r, carry):
        start = jnp.clip(r - kh // 2, 0, rows - kh)
        q_r = q_ref[pl.ds(pl.multiple_of(r * GRID_W, GRID_W), GRID_W), :]
        qm = _stack_heads(q_r)
        koff = pl.multiple_of(start * GRID_W, GRID_W)
        kb = k_ref[pl.ds(koff, band), :]
        vb = v_ref[pl.ds(koff, band), :]
        s_lat = lax.dot_general(qm, kb, _NT, preferred_element_type=F32) + bias_ref[start - r + kh - 1]
        s_ctx = lax.dot_general(qm, kc, _NT, preferred_element_type=F32)
        mx = jnp.maximum(jnp.max(s_lat, axis=-1, keepdims=True), jnp.max(s_ctx, axis=-1, keepdims=True))
        p_lat = jnp.exp(s_lat - mx)
        p_ctx = jnp.exp(s_ctx - mx)
        den = jnp.sum(p_lat, axis=-1, keepdims=True) + jnp.sum(p_ctx, axis=-1, keepdims=True)
        o = (jnp.dot(p_lat.astype(BF16), vb, preferred_element_type=F32)
             + jnp.dot(p_ctx.astype(BF16), vc, preferred_element_type=F32))
        o_ref[pl.ds(pl.multiple_of(r * GRID_W, GRID_W), GRID_W), :] = _merge_heads(o / den).astype(BF16)
        return carry

    lax.fori_loop(0, rows, row_body, 0)

    if need_ctx:
        oc_ref = out_refs[1]
        qm = _stack_heads(qc_ref[...])
        s = lax.dot_general(qm, kc, _NT, preferred_element_type=F32)
        p = jnp.exp(s - jnp.max(s, axis=-1, keepdims=True))
        den = jnp.sum(p, axis=-1, keepdims=True)
        o = jnp.dot(p.astype(BF16), vc, preferred_element_type=F32)
        oc_ref[...] = _merge_heads(o / den).astype(BF16)


def _attn_bias(rpb, rows):
    kh = min(WIN_H, rows)
    qc = np.arange(GRID_W)[:, None]
    kc = np.arange(GRID_W)[None, :]
    col_start = np.clip(qc - WIN_W // 2, 0, GRID_W - WIN_W)
    col_mask = (kc >= col_start) & (kc < col_start + WIN_W)
    dc_idx = np.clip(kc - qc, -(WIN_W - 1), WIN_W - 1) + (WIN_W - 1)
    tab = jnp.where(col_mask[None, None], rpb.astype(F32)[:, :, dc_idx], -jnp.inf)
    variants = [tab[:, s + WIN_H - kh:s + WIN_H] for s in range(kh)]
    b = jnp.stack(variants, axis=1)
    h = rpb.shape[0]
    b = b.reshape(h // HEADS_PER_BLOCK, HEADS_PER_BLOCK, kh, kh, GRID_W, GRID_W)
    b = b.transpose(0, 2, 1, 4, 3, 5)
    return b.reshape(h // HEADS_PER_BLOCK, kh, HEADS_PER_BLOCK * GRID_W, kh * GRID_W)


def _attention(qkv_x, qkv_c, bias, batch, need_ctx):
    tx, d3 = qkv_x.shape
    d = d3 // 3
    n = tx // batch
    l = qkv_c.shape[0] // batch
    rows = n // GRID_W
    kh = min(WIN_H, rows)
    npair = d // LANES
    kern = functools.partial(_attn_kernel, rows=rows, kh=kh, need_ctx=need_ctx)
    in_specs = [
        pl.BlockSpec((n, LANES), lambda p, b: (b, p)),
        pl.BlockSpec((n, LANES), lambda p, b: (b, npair + p)),
        pl.BlockSpec((n, LANES), lambda p, b: (b, 2 * npair + p)),
        pl.BlockSpec((l, LANES), lambda p, b: (b, p)),
        pl.BlockSpec((l, LANES), lambda p, b: (b, npair + p)),
        pl.BlockSpec((l, LANES), lambda p, b: (b, 2 * npair + p)),
        pl.BlockSpec((None,) + bias.shape[1:], lambda p, b: (p, 0, 0, 0)),
    ]
    out_shape = [jax.ShapeDtypeStruct((tx, d), BF16)]
    out_specs = [pl.BlockSpec((n, LANES), lambda p, b: (b, p))]
    if need_ctx:
        out_shape.append(jax.ShapeDtypeStruct((batch * l, d), BF16))
        out_specs.append(pl.BlockSpec((l, LANES), lambda p, b: (b, p)))
    est = 2 * (4 * n * LANES * 2 + 4 * l * LANES * 2 + int(np.prod(bias.shape[1:])) * 4) + 8 * 1024 * 1024
    outs = pl.pallas_call(
        kern,
        grid=(npair, batch),
        in_specs=in_specs,
        out_specs=out_specs,
        out_shape=out_shape,
        compiler_params=_params(est, 2),
        name="nbr_attention",
    )(qkv_x, qkv_x, qkv_x, qkv_c, qkv_c, qkv_c, bias)
    return (outs[0], outs[1]) if need_ctx else (outs[0], None)


def _proj_ffn_kernel(x_ref, a_ref, mod_ref, wo_ref, g_ref, w1_ref, w3_ref, w2_ref, *rest,
                     tm, tpg, gbase, fc, final):
    final_g_ref = rest[0] if final else None
    o_ref, hid_ref = rest[-2:]
    d = x_ref.shape[-1]
    m = _mod_row(mod_ref, tm, tpg, gbase)
    y = jnp.dot(a_ref[...], wo_ref[...], preferred_element_type=F32)
    x1 = x_ref[...] + _chunk(m, 2, d) * y
    _finish(_ffn_tail(x1, m, g_ref[...], w1_ref, w3_ref, w2_ref, hid_ref, fc), final_g_ref, o_ref)


def _conv_ffn_kernel(x_ref, z_ref, zp_ref, zn_ref, gb_ref, cw_ref, mod_ref, wo_ref, g_ref,
                     w1_ref, w3_ref, w2_ref, o_ref, hid_ref, *, tm, tpg, gbase, fc, seq):
    d = x_ref.shape[-1]
    i = pl.program_id(0)
    m = _mod_row(mod_ref, tm, tpg, gbase)
    pos0 = lax.rem(i * tm, seq)
    z = z_ref[...].astype(F32)
    prev = zp_ref[...].astype(F32)[BF16_SUBLANES - 1:, :]
    nxt = zn_ref[...].astype(F32)[:1, :]
    prev = jnp.where(pos0 == 0, jnp.zeros_like(prev), prev)
    nxt = jnp.where(pos0 + tm == seq, jnp.zeros_like(nxt), nxt)
    row = lax.broadcasted_iota(jnp.int32, z.shape, 0)
    z_m1 = jnp.where(row == 0, prev, pltpu.roll(z, 1, 0))
    z_p1 = jnp.where(row == tm - 1, nxt, pltpu.roll(z, tm - 1, 0))
    cw = cw_ref[...]
    conv = z_m1 * cw[0:1] + z * cw[1:2] + z_p1 * cw[2:3]
    a = (gb_ref[...].astype(F32) * conv).astype(BF16)
    y = jnp.dot(a, wo_ref[...], preferred_element_type=F32)
    x1 = x_ref[...] + _chunk(m, 2, d) * y
    _finish(_ffn_tail(x1, m, g_ref[...], w1_ref, w3_ref, w2_ref, hid_ref, fc), None, o_ref)


def _pool_ffn_kernel(x_ref, xp_ref, xn_ref, mod_ref, gm_ref, pw_ref, ps_ref, g_ref,
                     w1_ref, w3_ref, w2_ref, o_ref, hid_ref, hext_ref, *, tm, tpg, gbase, fc, seq):
    d = x_ref.shape[-1]
    halo = F32_SUBLANES
    i = pl.program_id(0)
    m = _mod_row(mod_ref, tm, tpg, gbase)
    shift, scale = _chunk(m, 0, d), _chunk(m, 1, d)
    gm = gm_ref[...]
    pos0 = lax.rem(i * tm, seq)
    x = x_ref[...]
    h_top = _norm_mod(xp_ref[...], gm, shift, scale)
    h_bot = _norm_mod(xn_ref[...], gm, shift, scale)
    hext_ref[0:halo, :] = jnp.where(pos0 == 0, jnp.zeros_like(h_top), h_top)
    hext_ref[halo:halo + tm, :] = _norm_mod(x, gm, shift, scale)
    hext_ref[halo + tm:, :] = jnp.where(pos0 + tm == seq, jnp.zeros_like(h_bot), h_bot)

    pos = pos0 + lax.broadcasted_iota(jnp.int32, (tm, 1), 0)
    gc = d // len(POOL_WINDOWS)
    pieces = []
    for gi, win in enumerate(POOL_WINDOWS):
        cols = slice(gi * gc, (gi + 1) * gc)
        acc = hext_ref[pl.ds(halo - win // 2, tm), cols]
        for dd in range(-win // 2 + 1, win // 2):
            acc = acc + hext_ref[pl.ds(halo + dd, tm), cols]
        cnt = (jnp.minimum(pos + win // 2, seq) - jnp.maximum(pos - win // 2, 0)).astype(F32)
        pooled = (acc / cnt - hext_ref[pl.ds(halo, tm), cols]).astype(BF16)
        pieces.append(jnp.dot(pooled, pw_ref[gi], preferred_element_type=F32))
    y = jnp.concatenate(pieces, axis=-1) * ps_ref[...]
    x1 = x + _chunk(m, 2, d) * y
    _finish(_ffn_tail(x1, m, g_ref[...], w1_ref, w3_ref, w2_ref, hid_ref, fc), None, o_ref)


def _ffn_est(tm, d, f):
    return 3 * d * f * 2 + 4 * tm * d * 4 + tm * f * 2 + 6 * tm * d * 4 + 3 * tm * 512 * 4


def _ffn_consts(g_ffn, w1, w3, w2):
    return ([g_ffn, w1, w3, w2],
            [_const_spec(g_ffn.shape), _const_spec(w1.shape), _const_spec(w3.shape), _const_spec(w2.shape)])


def _proj_ffn(x, a, mod, wo, ffn, tm, tpg, gbase, fc, final_g=None):
    t, d = x.shape
    f = ffn[1].shape[1]
    tile = pl.BlockSpec((tm, d), lambda i: (i, 0))
    fargs, fspecs = _ffn_consts(*ffn)
    args = [x, a, mod, wo] + fargs
    specs = [tile, tile, _const_spec(mod.shape), _const_spec(wo.shape)] + fspecs
    if final_g is not None:
        args.append(final_g)
        specs.append(_const_spec(final_g.shape))
    kern = functools.partial(_proj_ffn_kernel, tm=tm, tpg=tpg, gbase=gbase, fc=fc, final=final_g is not None)
    return pl.pallas_call(
        kern, grid=(t // tm,), in_specs=specs, out_specs=tile,
        out_shape=jax.ShapeDtypeStruct((t, d), F32),
        scratch_shapes=[pltpu.VMEM((tm, f), BF16)],
        compiler_params=_params(_ffn_est(tm, d, f) + d * d * 2 + 2 * tm * d * 2),
        name="proj_ffn",
    )(*args)


def _conv_ffn(x, z, gb, conv_w, mod, wo, ffn, tm, tpg, gbase, fc, seq):
    t, d = x.shape
    f = ffn[1].shape[1]
    tile = pl.BlockSpec((tm, d), lambda i: (i, 0))
    hb = tm // BF16_SUBLANES
    nhb = t // BF16_SUBLANES
    prev = pl.BlockSpec((BF16_SUBLANES, d), lambda i: (jnp.maximum(i * hb - 1, 0), 0))
    nxt = pl.BlockSpec((BF16_SUBLANES, d), lambda i: (jnp.minimum((i + 1) * hb, nhb - 1), 0))
    fargs, fspecs = _ffn_consts(*ffn)
    kern = functools.partial(_conv_ffn_kernel, tm=tm, tpg=tpg, gbase=gbase, fc=fc, seq=seq)
    return pl.pallas_call(
        kern, grid=(t // tm,),
        in_specs=[tile, tile, prev, nxt, tile, _const_spec(conv_w.shape), _const_spec(mod.shape),
                  _const_spec(wo.shape)] + fspecs,
        out_specs=tile,
        out_shape=jax.ShapeDtypeStruct((t, d), F32),
        scratch_shapes=[pltpu.VMEM((tm, f), BF16)],
        compiler_params=_params(_ffn_est(tm, d, f) + d * d * 2 + 4 * tm * d * 2 + 4 * tm * d * 4),
        name="conv_ffn",
    )(x, z, z, z, gb, conv_w, mod, wo, *fargs)


def _pool_ffn(x, mod, g_mix, pool_w, pool_scale, ffn, tm, tpg, gbase, fc, seq):
    t, d = x.shape
    f = ffn[1].shape[1]
    tile = pl.BlockSpec((tm, d), lambda i: (i, 0))
    hb = tm // F32_SUBLANES
    nhb = t // F32_SUBLANES
    prev = pl.BlockSpec((F32_SUBLANES, d), lambda i: (jnp.maximum(i * hb - 1, 0), 0))
    nxt = pl.BlockSpec((F32_SUBLANES, d), lambda i: (jnp.minimum((i + 1) * hb, nhb - 1), 0))
    fargs, fspecs = _ffn_consts(*ffn)
    kern = functools.partial(_pool_ffn_kernel, tm=tm, tpg=tpg, gbase=gbase, fc=fc, seq=seq)
    return pl.pallas_call(
        kern, grid=(t // tm,),
        in_specs=[tile, prev, nxt, _const_spec(mod.shape), _const_spec(g_mix.shape),
                  _const_spec(pool_w.shape), _const_spec(pool_scale.shape)] + fspecs,
        out_specs=tile,
        out_shape=jax.ShapeDtypeStruct((t, d), F32),
        scratch_shapes=[pltpu.VMEM((tm, f), BF16), pltpu.VMEM((tm + 2 * F32_SUBLANES, d), F32)],
        compiler_params=_params(_ffn_est(tm, d, f) + 3 * tm * d * 4),
        name="pool_ffn",
    )(x, x, x, mod, g_mix, pool_w, pool_scale, *fargs)


def _token_tile(seq):
    return min(seq, 512)


def kernel(x, c, ctx, c_ctx, adaln_w, adaln_b, norm_mix_g, norm_ffn_g, ffn_w1, ffn_w3, ffn_w2,
           na_w_qkv, na_w_o, na_rpb, sc_w_in, sc_conv_w, sc_w_out, pool_w, pool_scale, final_g):
    batch, n, d = x.shape
    l = ctx.shape[1]
    depth = adaln_w.shape[0]
    f = ffn_w1.shape[-1]
    assert batch + 1 <= MOD_ROWS and d % LANES == 0 and n % GRID_W == 0
    assert d // N_HEADS * HEADS_PER_BLOCK == LANES
    fc = 256
    assert f % fc == 0
    assert (depth - 1) % N_MIXERS == 0, "the final norm is fused into the attention-layer tail"
    tm_x, tm_c = _token_tile(n), _token_tile(l)
    q_scale = float(d // N_HEADS) ** -0.5

    c_rows = jnp.concatenate([c, c_ctx[None, :], jnp.zeros((MOD_ROWS - batch - 1, d), F32)], axis=0)
    mod_all = _modulation(c_rows, adaln_w, adaln_b)

    xs = x.reshape(batch * n, d)
    cs = ctx.reshape(batch * l, d)
    x_grp = dict(tm=tm_x, tpg=n, gbase=0)
    c_grp = dict(tm=tm_c, tpg=batch * l, gbase=batch)

    for layer in range(depth):
        kind = layer % N_MIXERS
        j = layer // N_MIXERS
        need_ctx = layer < depth - 1
        mod = mod_all[layer]
        g_mix = norm_mix_g[layer].reshape(1, d)
        ffn = (norm_ffn_g[layer].reshape(1, d), ffn_w1[layer].astype(BF16), ffn_w3[layer].astype(BF16),
               ffn_w2[layer].astype(BF16))
        fin = final_g.reshape(1, d) if layer == depth - 1 else None

        if kind == 0:
            wqkv = na_w_qkv[j].astype(BF16)
            wo = na_w_o[j].astype(BF16)
            bias = _attn_bias(na_rpb[j], n // GRID_W)
            qkv_x = _qkv_proj(xs, mod, g_mix, wqkv, q_scale=q_scale, **x_grp)
            qkv_c = _qkv_proj(cs, mod, g_mix, wqkv, q_scale=q_scale, **c_grp)
            o_x, o_c = _attention(qkv_x, qkv_c, bias, batch, need_ctx)
            xs = _proj_ffn(xs, o_x, mod, wo, ffn, fc=fc, final_g=fin, **x_grp)
            if need_ctx:
                cs = _proj_ffn(cs, o_c, mod, wo, ffn, fc=fc, **c_grp)
        elif kind == 1:
            w_in = sc_w_in[j].astype(BF16)
            wo = sc_w_out[j].astype(BF16)
            gb_x, z_x = _conv_in_proj(xs, mod, g_mix, w_in, **x_grp)
            xs_new = _conv_ffn(xs, z_x, gb_x, sc_conv_w[j], mod, wo, ffn, fc=fc, seq=n, **x_grp)
            if need_ctx:
                gb_c, z_c = _conv_in_proj(cs, mod, g_mix, w_in, **c_grp)
                cs = _conv_ffn(cs, z_c, gb_c, sc_conv_w[j], mod, wo, ffn, fc=fc, seq=l, **c_grp)
            xs = xs_new
        else:
            pw = pool_w[j].astype(BF16)
            ps = pool_scale[j].reshape(1, d)
            xs_new = _pool_ffn(xs, mod, g_mix, pw, ps, ffn, fc=fc, seq=n, **x_grp)
            if need_ctx:
                cs = _pool_ffn(cs, mod, g_mix, pw, ps, ffn, fc=fc, seq=l, **c_grp)
            xs = xs_new
    return xs.reshape(batch, n, d)
```

```python
import functools

import numpy as np
import jax
import jax.numpy as jnp
from jax import lax
from jax.experimental import pallas as pl
from jax.experimental.pallas import tpu as pltpu

F32 = jnp.float32
BF16 = jnp.bfloat16

GRID_W = 64
N_MIXERS = 3
N_HEADS = 16
WIN_H = 8
WIN_W = 16
CONV_W = 3
POOL_WINDOWS = (2, 4, 8, 16)
N_MOD = 6
EPS = 1e-6

V7X_VMEM_BYTES = 64 * 1024 * 1024
LANES = 128
F32_SUBLANES = 8
BF16_SUBLANES = 16

HEADS_PER_BLOCK = 2
MOD_ROWS = 8


def _vmem_limit(est_bytes):
    return int(min(V7X_VMEM_BYTES - 6 * 1024 * 1024, max(32 * 1024 * 1024, est_bytes * 5 // 4)))


def _params(est_bytes, ndims=1):
    return pltpu.CompilerParams(dimension_semantics=("arbitrary",) * ndims,
                                vmem_limit_bytes=_vmem_limit(est_bytes))


def _const_spec(shape):
    nd = len(shape)
    return pl.BlockSpec(shape, lambda *_: (0,) * nd, pipeline_mode=pl.Buffered(1))


def _split_bf16(v):
    hi = v.astype(BF16)
    lo = (v - hi.astype(F32)).astype(BF16)
    return hi, lo


def _mod_kernel(c_ref, w_ref, b_ref, o_ref):
    c = c_ref[...]
    s = c * jax.nn.sigmoid(c)
    s_hi, s_lo = _split_bf16(s)
    lhs = jnp.concatenate([s_hi, s_lo], axis=0)
    w_hi, w_lo = _split_bf16(w_ref[...])
    r = (jnp.dot(lhs, w_hi, preferred_element_type=F32)
         + jnp.dot(lhs, w_lo, preferred_element_type=F32))
    o_ref[...] = r[:MOD_ROWS] + r[MOD_ROWS:] + b_ref[...]


def _modulation(c_rows, adaln_w, adaln_b):
    depth, d, nm = adaln_w.shape
    nc = nm // 4
    est = 2 * d * nc * 4 + 3 * d * nc * 2 + 4 * MOD_ROWS * nc * 4
    return pl.pallas_call(
        _mod_kernel,
        grid=(depth, nm // nc),
        in_specs=[
            pl.BlockSpec((MOD_ROWS, d), lambda l, j: (0, 0)),
            pl.BlockSpec((None, d, nc), lambda l, j: (l, 0, j)),
            pl.BlockSpec((None, 1, nc), lambda l, j: (l, 0, j)),
        ],
        out_specs=pl.BlockSpec((None, MOD_ROWS, nc), lambda l, j: (l, 0, j)),
        out_shape=jax.ShapeDtypeStruct((depth, MOD_ROWS, nm), F32),
        compiler_params=_params(est, 2),
        name="adaln_mod",
    )(c_rows, adaln_w, adaln_b.reshape(depth, 1, nm))


def _mod_row(mod_ref, tm, tokens_per_group, group_base):
    grp = group_base + lax.div(pl.program_id(0) * tm, tokens_per_group)
    return mod_ref[pl.ds(grp, 1), :]


def _chunk(m, idx, d):
    return m[:, idx * d:(idx + 1) * d]


def _rms(x, g):
    ms = jnp.mean(x * x, axis=-1, keepdims=True)
    return x * lax.rsqrt(ms + EPS) * g


def _norm_mod(x, g, shift, scale):
    return _rms(x, g) * (1.0 + scale) + shift


def _ffn_tail(x1, m, g_ffn, w1_ref, w3_ref, w2_ref, hid_ref, fc):
    d = x1.shape[-1]
    f = w1_ref.shape[-1]
    h = _norm_mod(x1, g_ffn, _chunk(m, 3, d), _chunk(m, 4, d)).astype(BF16)
    for c in range(f // fc):
        sl = slice(c * fc, (c + 1) * fc)
        a = jnp.dot(h, w1_ref[:, sl], preferred_element_type=F32)
        b = jnp.dot(h, w3_ref[:, sl], preferred_element_type=F32)
        hid_ref[:, sl] = (a * jax.nn.sigmoid(a) * b).astype(BF16)
    y = jnp.dot(hid_ref[...], w2_ref[...], preferred_element_type=F32)
    return x1 + _chunk(m, 5, d) * y


def _finish(out, final_g_ref, o_ref):
    if final_g_ref is not None:
        out = _rms(out, final_g_ref[...])
    o_ref[...] = out


def _qkv_kernel(x_ref, mod_ref, g_ref, w_ref, o_ref, *, tm, tpg, gbase, q_scale):
    d = x_ref.shape[-1]
    m = _mod_row(mod_ref, tm, tpg, gbase)
    h = _norm_mod(x_ref[...], g_ref[...], _chunk(m, 0, d), _chunk(m, 1, d)).astype(BF16)
    for j in range(3):
        r = jnp.dot(h, w_ref[:, j * d:(j + 1) * d], preferred_element_type=F32)
        if j == 0:
            r = r * q_scale
        o_ref[:, j * d:(j + 1) * d] = r.astype(BF16)


def _conv_in_kernel(x_ref, mod_ref, g_ref, w_ref, gb_ref, z_ref, *, tm, tpg, gbase):
    d = x_ref.shape[-1]
    m = _mod_row(mod_ref, tm, tpg, gbase)
    h = _norm_mod(x_ref[...], g_ref[...], _chunk(m, 0, d), _chunk(m, 1, d)).astype(BF16)
    gb_ref[...] = jnp.dot(h, w_ref[:, :d], preferred_element_type=F32).astype(BF16)
    gc = jnp.dot(h, w_ref[:, d:2 * d], preferred_element_type=F32)
    val = jnp.dot(h, w_ref[:, 2 * d:], preferred_element_type=F32)
    z_ref[...] = (gc * val).astype(BF16)


def _front_call(kernel, x, mod, g, w, out_shapes, out_specs, tm, name):
    t, d = x.shape
    est = 2 * tm * d * 4 + w.size * 2 + 2 * tm * w.shape[1] * 2 + tm * w.shape[1] * 4 + 4 * tm * d * 4
    return pl.pallas_call(
        kernel,
        grid=(t // tm,),
        in_specs=[
            pl.BlockSpec((tm, d), lambda i: (i, 0)),
            _const_spec(mod.shape),
            _const_spec(g.shape),
            _const_spec(w.shape),
        ],
        out_specs=out_specs,
        out_shape=out_shapes,
        compiler_params=_params(est),
        name=name,
    )(x, mod, g, w)


def _qkv_proj(x, mod, g, w, tm, tpg, gbase, q_scale):
    t, d = x.shape
    kern = functools.partial(_qkv_kernel, tm=tm, tpg=tpg, gbase=gbase, q_scale=q_scale)
    return _front_call(kern, x, mod, g, w,
                       jax.ShapeDtypeStruct((t, 3 * d), BF16),
                       pl.BlockSpec((tm, 3 * d), lambda i: (i, 0)), tm, "qkv_proj")


def _conv_in_proj(x, mod, g, w, tm, tpg, gbase):
    t, d = x.shape
    kern = functools.partial(_conv_in_kernel, tm=tm, tpg=tpg, gbase=gbase)
    spec = pl.BlockSpec((tm, d), lambda i: (i, 0))
    return _front_call(kern, x, mod, g, w,
                       (jax.ShapeDtypeStruct((t, d), BF16), jax.ShapeDtypeStruct((t, d), BF16)),
                       (spec, spec), tm, "conv_in_proj")


def _stack_heads(q):
    lane = lax.broadcasted_iota(jnp.int32, q.shape, 1)
    zero = jnp.zeros_like(q)
    return jnp.concatenate([jnp.where(lane < LANES // 2, q, zero),
                            jnp.where(lane >= LANES // 2, q, zero)], axis=0)


def _merge_heads(o):
    mrows = o.shape[0] // 2
    lane = lax.broadcasted_iota(jnp.int32, (mrows, o.shape[1]), 1)
    return jnp.where(lane < LANES // 2, o[:mrows], o[mrows:])


_NT = (((1,), (1,)), ((), ()))


QROWS = 4


def _band_plan(rows, kh):
    band = QROWS + kh
    assert rows % QROWS == 0 and band % 2 == 0 and rows >= band + QROWS

    def plan(r0):
        ub = int(np.clip(r0 - kh // 2, 0, rows - band))
        tab = np.full((QROWS, band), -1, np.int64)
        for ri in range(QROWS):
            r = r0 + ri
            start = int(np.clip(r - kh // 2, 0, rows - kh))
            for u in range(band):
                if start <= ub + u < start + kh:
                    tab[ri, u] = ub + u - r + WIN_H - 1
        return ub - r0, tab

    first, last = plan(0), plan(rows - QROWS)
    interior = plan(QROWS)
    for r0 in range(QROWS, rows - QROWS, QROWS):
        off, tab = plan(r0)
        assert off == interior[0] and (tab == interior[1]).all()
    return band, [first, interior, last]


def _build_bias(rp_ref, xm_ref, bias_ref, plans):
    half = LANES // 2
    shape = (GRID_W, LANES)
    qcol = lax.broadcasted_iota(jnp.int32, shape, 0)
    lane = lax.broadcasted_iota(jnp.int32, shape, 1)
    kcol = jnp.bitwise_and(lane, half - 1)
    col_start = jnp.clip(qcol - WIN_W // 2, 0, GRID_W - WIN_W)
    in_window = (kcol >= col_start) & (kcol < col_start + WIN_W)
    low = lane < half
    neg = jnp.full(shape, -jnp.inf, F32)
    n_dr = rp_ref.shape[1]
    for hh in range(HEADS_PER_BLOCK):
        for dr in range(n_dr):
            base = jnp.broadcast_to(rp_ref[hh, dr:dr + 1, :], shape)
            t_lo = pltpu.roll(base, 0, 1, stride=1, stride_axis=0)
            both = jnp.where(low, t_lo, pltpu.roll(t_lo, half, 1))
            xm_ref[hh, dr] = jnp.where(in_window, both, neg)
    for v, (_, tab) in enumerate(plans):
        for hh in range(HEADS_PER_BLOCK):
            for ri in range(QROWS):
                r_lo = (hh * QROWS + ri) * GRID_W
                for j in range(tab.shape[1] // 2):
                    da, db = int(tab[ri, 2 * j]), int(tab[ri, 2 * j + 1])
                    ta = xm_ref[hh, da] if da >= 0 else neg
                    tb = xm_ref[hh, db] if db >= 0 else neg
                    bias_ref[v, r_lo:r_lo + GRID_W, j * LANES:(j + 1) * LANES] = jnp.where(low, ta, tb)


SOFTMAX_SLAB = 16
BLOCKS_PER_STEP = 2


def _softmax_slabs(s_parts, biases, p_ref):
    nrows = s_parts[0].shape[0]
    dens = []
    for r in range(0, nrows, SOFTMAX_SLAB):
        rs = slice(r, r + SOFTMAX_SLAB)
        parts = [s[rs] if b is None else s[rs] + b[rs, :] for s, b in zip(s_parts, biases)]
        blocks = [p[:, c:c + LANES] for p in parts for c in range(0, p.shape[1], LANES)]
        mx = jnp.max(functools.reduce(jnp.maximum, blocks), axis=-1, keepdims=True)
        es = [jnp.exp(b - mx) for b in blocks]
        dens.append(jnp.sum(functools.reduce(jnp.add, es), axis=-1, keepdims=True))
        for j, e in enumerate(es):
            p_ref[rs, j * LANES:(j + 1) * LANES] = e.astype(BF16)
    return jnp.concatenate(dens, axis=0)


def _attn_kernel(q_ref, k_ref, v_ref, qc_ref, kc_ref, vc_ref, rp_ref, *rest, rows, kh, need_ctx):
    o_ref = rest[0]
    oc_ref = rest[1] if need_ctx else None
    xm_ref, bias_ref = rest[-2 - BLOCKS_PER_STEP:-BLOCKS_PER_STEP]
    p_refs = rest[-BLOCKS_PER_STEP:]
    band, plans = _band_plan(rows, kh)
    nblk = rows // QROWS
    qn = QROWS * GRID_W
    kn = band * GRID_W

    @pl.when(pl.program_id(1) == 0)
    def _():
        _build_bias(rp_ref, xm_ref, bias_ref, plans)

    kc = kc_ref[...]
    vc = vc_ref[...]

    def scores(t):
        r0 = t * QROWS
        ub = jnp.clip(r0 - kh // 2, 0, rows - band)
        variant = (t > 0).astype(jnp.int32) + (t == nblk - 1).astype(jnp.int32)
        qoff = pl.multiple_of(r0 * GRID_W, qn)
        koff = pl.multiple_of(ub * GRID_W, GRID_W)
        qm = _stack_heads(q_ref[pl.ds(qoff, qn), :])
        s_lat = lax.dot_general(qm, k_ref[pl.ds(koff, kn), :], _NT, preferred_element_type=F32)
        s_ctx = lax.dot_general(qm, kc, _NT, preferred_element_type=F32)
        return qoff, koff, variant, s_lat, s_ctx

    def finish(blk, p_ref):
        qoff, koff, variant, s_lat, s_ctx = blk
        den = _softmax_slabs([s_lat, s_ctx], [bias_ref.at[variant], None], p_ref)
        o = (jnp.dot(p_ref[:, :kn], v_ref[pl.ds(koff, kn), :], preferred_element_type=F32)
             + jnp.dot(p_ref[:, kn:], vc, preferred_element_type=F32))
        o_ref[pl.ds(qoff, qn), :] = _merge_heads(o / den).astype(BF16)

    def step(tt, carry):
        blocks = [scores(tt * BLOCKS_PER_STEP + i) for i in range(BLOCKS_PER_STEP)]
        for blk, p_ref in zip(blocks, p_refs):
            finish(blk, p_ref)
        return carry

    lax.fori_loop(0, nblk // BLOCKS_PER_STEP, step, 0)

    if need_ctx:
        lc = kc.shape[0]
        p_ref = p_refs[0]
        s = lax.dot_general(_stack_heads(qc_ref[...]), kc, _NT, preferred_element_type=F32)
        den = _softmax_slabs([s], [None], p_ref)
        o = jnp.dot(p_ref[:s.shape[0], :lc], vc, preferred_element_type=F32)
        oc_ref[...] = _merge_heads(o / den).astype(BF16)


def _rpb_rows(rpb):
    h, n_dr, _ = rpb.shape
    gap = jnp.zeros((h, n_dr, LANES - (2 * WIN_W - 1)), F32)
    return jnp.concatenate([rpb[..., WIN_W - 1:].astype(F32), gap, rpb[..., :WIN_W - 1].astype(F32)], axis=-1)


def _attention(qkv_x, qkv_c, rp, batch, need_ctx):
    tx, d3 = qkv_x.shape
    d = d3 // 3
    n = tx // batch
    l = qkv_c.shape[0] // batch
    rows = n // GRID_W
    kh = min(WIN_H, rows)
    npair = d // LANES
    band, plans = _band_plan(rows, kh)
    n_dr = rp.shape[1]
    kern = functools.partial(_attn_kernel, rows=rows, kh=kh, need_ctx=need_ctx)
    in_specs = [
        pl.BlockSpec((n, LANES), lambda p, b: (b, p)),
        pl.BlockSpec((n, LANES), lambda p, b: (b, npair + p)),
        pl.BlockSpec((n, LANES), lambda p, b: (b, 2 * npair + p)),
        pl.BlockSpec((l, LANES), lambda p, b: (b, p)),
        pl.BlockSpec((l, LANES), lambda p, b: (b, npair + p)),
        pl.BlockSpec((l, LANES), lambda p, b: (b, 2 * npair + p)),
        pl.BlockSpec((HEADS_PER_BLOCK, n_dr, LANES), lambda p, b: (p, 0, 0)),
    ]
    out_shape = [jax.ShapeDtypeStruct((tx, d), BF16)]
    out_specs = [pl.BlockSpec((n, LANES), lambda p, b: (b, p))]
    if need_ctx:
        out_shape.append(jax.ShapeDtypeStruct((batch * l, d), BF16))
        out_specs.append(pl.BlockSpec((l, LANES), lambda p, b: (b, p)))
    qrows = HEADS_PER_BLOCK * QROWS * GRID_W
    assert (rows // QROWS) % BLOCKS_PER_STEP == 0 and HEADS_PER_BLOCK * l <= qrows
    bias_bytes = len(plans) * qrows * band * GRID_W * 4
    est = (2 * (4 * n * LANES * 2 + 4 * l * LANES * 2) + bias_bytes
           + HEADS_PER_BLOCK * n_dr * GRID_W * LANES * 4 + 6 * qrows * (band * GRID_W + l) * 4)
    outs = pl.pallas_call(
        kern,
        grid=(npair, batch),
        in_specs=in_specs,
        out_specs=out_specs,
        out_shape=out_shape,
        scratch_shapes=[pltpu.VMEM((HEADS_PER_BLOCK, n_dr, GRID_W, LANES), F32),
                        pltpu.VMEM((len(plans), qrows, band * GRID_W), F32)]
        + [pltpu.VMEM((qrows, band * GRID_W + l), BF16)] * BLOCKS_PER_STEP,
        compiler_params=_params(est, 2),
        name="nbr_attention",
    )(qkv_x, qkv_x, qkv_x, qkv_c, qkv_c, qkv_c, rp)
    return (outs[0], outs[1]) if need_ctx else (outs[0], None)


def _proj_ffn_kernel(x_ref, a_ref, mod_ref, wo_ref, g_ref, w1_ref, w3_ref, w2_ref, *rest,
                     tm, tpg, gbase, fc, final):
    final_g_ref = rest[0] if final else None
    o_ref, hid_ref = rest[-2:]
    d = x_ref.shape[-1]
    m = _mod_row(mod_ref, tm, tpg, gbase)
    y = jnp.dot(a_ref[...], wo_ref[...], preferred_element_type=F32)
    x1 = x_ref[...] + _chunk(m, 2, d) * y
    _finish(_ffn_tail(x1, m, g_ref[...], w1_ref, w3_ref, w2_ref, hid_ref, fc), final_g_ref, o_ref)


def _conv_ffn_kernel(x_ref, z_ref, zp_ref, zn_ref, gb_ref, cw_ref, mod_ref, wo_ref, g_ref,
                     w1_ref, w3_ref, w2_ref, o_ref, hid_ref, *, tm, tpg, gbase, fc, seq):
    d = x_ref.shape[-1]
    i = pl.program_id(0)
    m = _mod_row(mod_ref, tm, tpg, gbase)
    pos0 = lax.rem(i * tm, seq)
    z = z_ref[...].astype(F32)
    prev = zp_ref[...].astype(F32)[BF16_SUBLANES - 1:, :]
    nxt = zn_ref[...].astype(F32)[:1, :]
    prev = jnp.where(pos0 == 0, jnp.zeros_like(prev), prev)
    nxt = jnp.where(pos0 + tm == seq, jnp.zeros_like(nxt), nxt)
    row = lax.broadcasted_iota(jnp.int32, z.shape, 0)
    z_m1 = jnp.where(row == 0, prev, pltpu.roll(z, 1, 0))
    z_p1 = jnp.where(row == tm - 1, nxt, pltpu.roll(z, tm - 1, 0))
    cw = cw_ref[...]
    conv = z_m1 * cw[0:1] + z * cw[1:2] + z_p1 * cw[2:3]
    a = (gb_ref[...].astype(F32) * conv).astype(BF16)
    y = jnp.dot(a, wo_ref[...], preferred_element_type=F32)
    x1 = x_ref[...] + _chunk(m, 2, d) * y
    _finish(_ffn_tail(x1, m, g_ref[...], w1_ref, w3_ref, w2_ref, hid_ref, fc), None, o_ref)


def _pool_ffn_kernel(x_ref, xp_ref, xn_ref, mod_ref, gm_ref, pw_ref, ps_ref, g_ref,
                     w1_ref, w3_ref, w2_ref, o_ref, hid_ref, hext_ref, *, tm, tpg, gbase, fc, seq):
    d = x_ref.shape[-1]
    halo = F32_SUBLANES
    i = pl.program_id(0)
    m = _mod_row(mod_ref, tm, tpg, gbase)
    shift, scale = _chunk(m, 0, d), _chunk(m, 1, d)
    gm = gm_ref[...]
    pos0 = lax.rem(i * tm, seq)
    x = x_ref[...]
    h_top = _norm_mod(xp_ref[...], gm, shift, scale)
    h_bot = _norm_mod(xn_ref[...], gm, shift, scale)
    hext_ref[0:halo, :] = jnp.where(pos0 == 0, jnp.zeros_like(h_top), h_top)
    hext_ref[halo:halo + tm, :] = _norm_mod(x, gm, shift, scale)
    hext_ref[halo + tm:, :] = jnp.where(pos0 + tm == seq, jnp.zeros_like(h_bot), h_bot)

    pos = pos0 + lax.broadcasted_iota(jnp.int32, (tm, 1), 0)
    gc = d // len(POOL_WINDOWS)
    pieces = []
    for gi, win in enumerate(POOL_WINDOWS):
        cols = slice(gi * gc, (gi + 1) * gc)
        acc = hext_ref[pl.ds(halo - win // 2, tm), cols]
        for dd in range(-win // 2 + 1, win // 2):
            acc = acc + hext_ref[pl.ds(halo + dd, tm), cols]
        cnt = (jnp.minimum(pos + win // 2, seq) - jnp.maximum(pos - win // 2, 0)).astype(F32)
        pooled = (acc / cnt - hext_ref[pl.ds(halo, tm), cols]).astype(BF16)
        pieces.append(jnp.dot(pooled, pw_ref[gi], preferred_element_type=F32))
    y = jnp.concatenate(pieces, axis=-1) * ps_ref[...]
    x1 = x + _chunk(m, 2, d) * y
    _finish(_ffn_tail(x1, m, g_ref[...], w1_ref, w3_ref, w2_ref, hid_ref, fc), None, o_ref)


def _ffn_est(tm, d, f):
    return 3 * d * f * 2 + 4 * tm * d * 4 + tm * f * 2 + 6 * tm * d * 4 + 3 * tm * 512 * 4


def _ffn_consts(g_ffn, w1, w3, w2):
    return ([g_ffn, w1, w3, w2],
            [_const_spec(g_ffn.shape), _const_spec(w1.shape), _const_spec(w3.shape), _const_spec(w2.shape)])


def _proj_ffn(x, a, mod, wo, ffn, tm, tpg, gbase, fc, final_g=None):
    t, d = x.shape
    f = ffn[1].shape[1]
    tile = pl.BlockSpec((tm, d), lambda i: (i, 0))
    fargs, fspecs = _ffn_consts(*ffn)
    args = [x, a, mod, wo] + fargs
    specs = [tile, tile, _const_spec(mod.shape), _const_spec(wo.shape)] + fspecs
    if final_g is not None:
        args.append(final_g)
        specs.append(_const_spec(final_g.shape))
    kern = functools.partial(_proj_ffn_kernel, tm=tm, tpg=tpg, gbase=gbase, fc=fc, final=final_g is not None)
    return pl.pallas_call(
        kern, grid=(t // tm,), in_specs=specs, out_specs=tile,
        out_shape=jax.ShapeDtypeStruct((t, d), F32),
        scratch_shapes=[pltpu.VMEM((tm, f), BF16)],
        compiler_params=_params(_ffn_est(tm, d, f) + d * d * 2 + 2 * tm * d * 2),
        name="proj_ffn",
    )(*args)


def _conv_ffn(x, z, gb, conv_w, mod, wo, ffn, tm, tpg, gbase, fc, seq):
    t, d = x.shape
    f = ffn[1].shape[1]
    tile = pl.BlockSpec((tm, d), lambda i: (i, 0))
    hb = tm // BF16_SUBLANES
    nhb = t // BF16_SUBLANES
    prev = pl.BlockSpec((BF16_SUBLANES, d), lambda i: (jnp.maximum(i * hb - 1, 0), 0))
    nxt = pl.BlockSpec((BF16_SUBLANES, d), lambda i: (jnp.minimum((i + 1) * hb, nhb - 1), 0))
    fargs, fspecs = _ffn_consts(*ffn)
    kern = functools.partial(_conv_ffn_kernel, tm=tm, tpg=tpg, gbase=gbase, fc=fc, seq=seq)
    return pl.pallas_call(
        kern, grid=(t // tm,),
        in_specs=[tile, tile, prev, nxt, tile, _const_spec(conv_w.shape), _const_spec(mod.shape),
                  _const_spec(wo.shape)] + fspecs,
        out_specs=tile,
        out_shape=jax.ShapeDtypeStruct((t, d), F32),
        scratch_shapes=[pltpu.VMEM((tm, f), BF16)],
        compiler_params=_params(_ffn_est(tm, d, f) + d * d * 2 + 4 * tm * d * 2 + 4 * tm * d * 4),
        name="conv_ffn",
    )(x, z, z, z, gb, conv_w, mod, wo, *fargs)


def _pool_ffn(x, mod, g_mix, pool_w, pool_scale, ffn, tm, tpg, gbase, fc, seq):
    t, d = x.shape
    f = ffn[1].shape[1]
    tile = pl.BlockSpec((tm, d), lambda i: (i, 0))
    hb = tm // F32_SUBLANES
    nhb = t // F32_SUBLANES
    prev = pl.BlockSpec((F32_SUBLANES, d), lambda i: (jnp.maximum(i * hb - 1, 0), 0))
    nxt = pl.BlockSpec((F32_SUBLANES, d), lambda i: (jnp.minimum((i + 1) * hb, nhb - 1), 0))
    fargs, fspecs = _ffn_consts(*ffn)
    kern = functools.partial(_pool_ffn_kernel, tm=tm, tpg=tpg, gbase=gbase, fc=fc, seq=seq)
    return pl.pallas_call(
        kern, grid=(t // tm,),
        in_specs=[tile, prev, nxt, _const_spec(mod.shape), _const_spec(g_mix.shape),
                  _const_spec(pool_w.shape), _const_spec(pool_scale.shape)] + fspecs,
        out_specs=tile,
        out_shape=jax.ShapeDtypeStruct((t, d), F32),
        scratch_shapes=[pltpu.VMEM((tm, f), BF16), pltpu.VMEM((tm + 2 * F32_SUBLANES, d), F32)],
        compiler_params=_params(_ffn_est(tm, d, f) + 3 * tm * d * 4),
        name="pool_ffn",
    )(x, x, x, mod, g_mix, pool_w, pool_scale, *fargs)


def _token_tile(seq):
    return min(seq, 512)


def kernel(x, c, ctx, c_ctx, adaln_w, adaln_b, norm_mix_g, norm_ffn_g, ffn_w1, ffn_w3, ffn_w2,
           na_w_qkv, na_w_o, na_rpb, sc_w_in, sc_conv_w, sc_w_out, pool_w, pool_scale, final_g):
    batch, n, d = x.shape
    l = ctx.shape[1]
    depth = adaln_w.shape[0]
    f = ffn_w1.shape[-1]
    assert batch + 1 <= MOD_ROWS and d % LANES == 0 and n % GRID_W == 0
    assert d // N_HEADS * HEADS_PER_BLOCK == LANES
    fc = 256
    assert f % fc == 0
    assert (depth - 1) % N_MIXERS == 0, "the final norm is fused into the attention-layer tail"
    tm_x, tm_c = _token_tile(n), _token_tile(l)
    q_scale = float(d // N_HEADS) ** -0.5

    c_rows = jnp.concatenate([c, c_ctx[None, :], jnp.zeros((MOD_ROWS - batch - 1, d), F32)], axis=0)
    mod_all = _modulation(c_rows, adaln_w, adaln_b)

    xs = x.reshape(batch * n, d)
    cs = ctx.reshape(batch * l, d)
    x_grp = dict(tm=tm_x, tpg=n, gbase=0)
    c_grp = dict(tm=tm_c, tpg=batch * l, gbase=batch)

    for layer in range(depth):
        kind = layer % N_MIXERS
        j = layer // N_MIXERS
        need_ctx = layer < depth - 1
        mod = mod_all[layer]
        g_mix = norm_mix_g[layer].reshape(1, d)
        ffn = (norm_ffn_g[layer].reshape(1, d), ffn_w1[layer].astype(BF16), ffn_w3[layer].astype(BF16),
               ffn_w2[layer].astype(BF16))
        fin = final_g.reshape(1, d) if layer == depth - 1 else None

        if kind == 0:
            wqkv = na_w_qkv[j].astype(BF16)
            wo = na_w_o[j].astype(BF16)
            rp = _rpb_rows(na_rpb[j])
            qkv_x = _qkv_proj(xs, mod, g_mix, wqkv, q_scale=q_scale, **x_grp)
            qkv_c = _qkv_proj(cs, mod, g_mix, wqkv, q_scale=q_scale, **c_grp)
            o_x, o_c = _attention(qkv_x, qkv_c, rp, batch, need_ctx)
            xs = _proj_ffn(xs, o_x, mod, wo, ffn, fc=fc, final_g=fin, **x_grp)
            if need_ctx:
                cs = _proj_ffn(cs, o_c, mod, wo, ffn, fc=fc, **c_grp)
        elif kind == 1:
            w_in = sc_w_in[j].astype(BF16)
            wo = sc_w_out[j].astype(BF16)
            gb_x, z_x = _conv_in_proj(xs, mod, g_mix, w_in, **x_grp)
            xs_new = _conv_ffn(xs, z_x, gb_x, sc_conv_w[j], mod, wo, ffn, fc=fc, seq=n, **x_grp)
            if need_ctx:
                gb_c, z_c = _conv_in_proj(cs, mod, g_mix, w_in, **c_grp)
                cs = _conv_ffn(cs, z_c, gb_c, sc_conv_w[j], mod, wo, ffn, fc=fc, seq=l, **c_grp)
            xs = xs_new
        else:
            pw = pool_w[j].astype(BF16)
            ps = pool_scale[j].reshape(1, d)
            xs_new = _pool_ffn(xs, mod, g_mix, pw, ps, ffn, fc=fc, seq=n, **x_grp)
            if need_ctx:
                cs = _pool_ffn(cs, mod, g_mix, pw, ps, ffn, fc=fc, seq=l, **c_grp)
            xs = xs_new
    return xs.reshape(batch, n, d)
```

```python
import functools

import numpy as np
import jax
import jax.numpy as jnp
from jax import lax
from jax.experimental import pallas as pl
from jax.experimental.pallas import tpu as pltpu

F32 = jnp.float32
BF16 = jnp.bfloat16

GRID_W = 64
N_MIXERS = 3
N_HEADS = 16
WIN_H = 8
WIN_W = 16
CONV_W = 3
POOL_WINDOWS = (2, 4, 8, 16)
N_MOD = 6
EPS = 1e-6
LOG2_E = 1.4426950408889634

V7X_VMEM_BYTES = 64 * 1024 * 1024
LANES = 128
F32_SUBLANES = 8
BF16_SUBLANES = 16

HEADS_PER_BLOCK = 2
MOD_ROWS = 8


def _vmem_limit(est_bytes):
    return int(min(V7X_VMEM_BYTES - 6 * 1024 * 1024, max(32 * 1024 * 1024, est_bytes * 5 // 4)))


def _params(est_bytes, ndims=1):
    return pltpu.CompilerParams(dimension_semantics=("arbitrary",) * ndims,
                                vmem_limit_bytes=_vmem_limit(est_bytes))


def _const_spec(shape):
    nd = len(shape)
    return pl.BlockSpec(shape, lambda *_: (0,) * nd, pipeline_mode=pl.Buffered(1))


def _split_bf16(v):
    hi = v.astype(BF16)
    lo = (v - hi.astype(F32)).astype(BF16)
    return hi, lo


def _mod_kernel(c_ref, w_ref, b_ref, o_ref):
    c = c_ref[...]
    s = c * jax.nn.sigmoid(c)
    s_hi, s_lo = _split_bf16(s)
    lhs = jnp.concatenate([s_hi, s_lo], axis=0)
    w_hi, w_lo = _split_bf16(w_ref[...])
    r = (jnp.dot(lhs, w_hi, preferred_element_type=F32)
         + jnp.dot(lhs, w_lo, preferred_element_type=F32))
    o_ref[...] = r[:MOD_ROWS] + r[MOD_ROWS:] + b_ref[...]


def _modulation(c_rows, adaln_w, adaln_b):
    depth, d, nm = adaln_w.shape
    nc = nm // 4
    est = 2 * d * nc * 4 + 3 * d * nc * 2 + 4 * MOD_ROWS * nc * 4
    return pl.pallas_call(
        _mod_kernel,
        grid=(depth, nm // nc),
        in_specs=[
            pl.BlockSpec((MOD_ROWS, d), lambda l, j: (0, 0)),
            pl.BlockSpec((None, d, nc), lambda l, j: (l, 0, j)),
            pl.BlockSpec((None, 1, nc), lambda l, j: (l, 0, j)),
        ],
        out_specs=pl.BlockSpec((None, MOD_ROWS, nc), lambda l, j: (l, 0, j)),
        out_shape=jax.ShapeDtypeStruct((depth, MOD_ROWS, nm), F32),
        compiler_params=_params(est, 2),
        name="adaln_mod",
    )(c_rows, adaln_w, adaln_b.reshape(depth, 1, nm))


def _mod_row(mod_ref, tm, tokens_per_group, group_base):
    grp = group_base + lax.div(pl.program_id(0) * tm, tokens_per_group)
    return mod_ref[pl.ds(grp, 1), :]


def _chunk(m, idx, d):
    return m[:, idx * d:(idx + 1) * d]


def _rms(x, g):
    ms = jnp.mean(x * x, axis=-1, keepdims=True)
    return x * lax.rsqrt(ms + EPS) * g


def _norm_mod(x, g, shift, scale):
    return _rms(x, g) * (1.0 + scale) + shift


def _ffn_tail(x1, m, g_ffn, w1_ref, w3_ref, w2_ref, hid_ref, fc):
    d = x1.shape[-1]
    f = w1_ref.shape[-1]
    h = _norm_mod(x1, g_ffn, _chunk(m, 3, d), _chunk(m, 4, d)).astype(BF16)
    for c in range(f // fc):
        sl = slice(c * fc, (c + 1) * fc)
        a = jnp.dot(h, w1_ref[:, sl], preferred_element_type=F32)
        b = jnp.dot(h, w3_ref[:, sl], preferred_element_type=F32)
        hid_ref[:, sl] = (a * jax.nn.sigmoid(a) * b).astype(BF16)
    y = jnp.dot(hid_ref[...], w2_ref[...], preferred_element_type=F32)
    return x1 + _chunk(m, 5, d) * y


def _finish(out, final_g_ref, o_ref):
    if final_g_ref is not None:
        out = _rms(out, final_g_ref[...])
    o_ref[...] = out


def _qkv_kernel(x_ref, mod_ref, g_ref, w_ref, o_ref, *, tm, tpg, gbase, q_scale):
    d = x_ref.shape[-1]
    m = _mod_row(mod_ref, tm, tpg, gbase)
    h = _norm_mod(x_ref[...], g_ref[...], _chunk(m, 0, d), _chunk(m, 1, d)).astype(BF16)
    for j in range(3):
        r = jnp.dot(h, w_ref[:, j * d:(j + 1) * d], preferred_element_type=F32)
        if j == 0:
            r = r * q_scale
        o_ref[:, j * d:(j + 1) * d] = r.astype(BF16)


def _conv_in_kernel(x_ref, mod_ref, g_ref, w_ref, gb_ref, z_ref, *, tm, tpg, gbase):
    d = x_ref.shape[-1]
    m = _mod_row(mod_ref, tm, tpg, gbase)
    h = _norm_mod(x_ref[...], g_ref[...], _chunk(m, 0, d), _chunk(m, 1, d)).astype(BF16)
    gb_ref[...] = jnp.dot(h, w_ref[:, :d], preferred_element_type=F32).astype(BF16)
    gc = jnp.dot(h, w_ref[:, d:2 * d], preferred_element_type=F32)
    val = jnp.dot(h, w_ref[:, 2 * d:], preferred_element_type=F32)
    z_ref[...] = (gc * val).astype(BF16)


def _front_call(kernel, x, mod, g, w, out_shapes, out_specs, tm, name):
    t, d = x.shape
    est = 2 * tm * d * 4 + w.size * 2 + 2 * tm * w.shape[1] * 2 + tm * w.shape[1] * 4 + 4 * tm * d * 4
    return pl.pallas_call(
        kernel,
        grid=(t // tm,),
        in_specs=[
            pl.BlockSpec((tm, d), lambda i: (i, 0)),
            _const_spec(mod.shape),
            _const_spec(g.shape),
            _const_spec(w.shape),
        ],
        out_specs=out_specs,
        out_shape=out_shapes,
        compiler_params=_params(est),
        name=name,
    )(x, mod, g, w)


def _qkv_proj(x, mod, g, w, tm, tpg, gbase, q_scale):
    t, d = x.shape
    kern = functools.partial(_qkv_kernel, tm=tm, tpg=tpg, gbase=gbase, q_scale=q_scale)
    return _front_call(kern, x, mod, g, w,
                       jax.ShapeDtypeStruct((t, 3 * d), BF16),
                       pl.BlockSpec((tm, 3 * d), lambda i: (i, 0)), tm, "qkv_proj")


def _conv_in_proj(x, mod, g, w, tm, tpg, gbase):
    t, d = x.shape
    kern = functools.partial(_conv_in_kernel, tm=tm, tpg=tpg, gbase=gbase)
    spec = pl.BlockSpec((tm, d), lambda i: (i, 0))
    return _front_call(kern, x, mod, g, w,
                       (jax.ShapeDtypeStruct((t, d), BF16), jax.ShapeDtypeStruct((t, d), BF16)),
                       (spec, spec), tm, "conv_in_proj")


def _stack_heads(q):
    lane = lax.broadcasted_iota(jnp.int32, q.shape, 1)
    zero = jnp.zeros_like(q)
    return jnp.concatenate([jnp.where(lane < LANES // 2, q, zero),
                            jnp.where(lane >= LANES // 2, q, zero)], axis=0)


def _merge_heads(o):
    mrows = o.shape[0] // 2
    lane = lax.broadcasted_iota(jnp.int32, (mrows, o.shape[1]), 1)
    return jnp.where(lane < LANES // 2, o[:mrows], o[mrows:])


_NT = (((1,), (1,)), ((), ()))


QROWS = 4


def _band_plan(rows, kh):
    band = QROWS + kh
    assert rows % QROWS == 0 and band % 2 == 0 and rows >= band + QROWS

    def plan(r0):
        ub = int(np.clip(r0 - kh // 2, 0, rows - band))
        tab = np.full((QROWS, band), -1, np.int64)
        for ri in range(QROWS):
            r = r0 + ri
            start = int(np.clip(r - kh // 2, 0, rows - kh))
            for u in range(band):
                if start <= ub + u < start + kh:
                    tab[ri, u] = ub + u - r + WIN_H - 1
        return ub - r0, tab

    first, last = plan(0), plan(rows - QROWS)
    interior = plan(QROWS)
    for r0 in range(QROWS, rows - QROWS, QROWS):
        off, tab = plan(r0)
        assert off == interior[0] and (tab == interior[1]).all()
    return band, [first, interior, last]


def _build_bias(rp_ref, xm_ref, bias_ref, plans):
    half = LANES // 2
    shape = (GRID_W, LANES)
    qcol = lax.broadcasted_iota(jnp.int32, shape, 0)
    lane = lax.broadcasted_iota(jnp.int32, shape, 1)
    kcol = jnp.bitwise_and(lane, half - 1)
    col_start = jnp.clip(qcol - WIN_W // 2, 0, GRID_W - WIN_W)
    in_window = (kcol >= col_start) & (kcol < col_start + WIN_W)
    low = lane < half
    neg = jnp.full(shape, -jnp.inf, F32)
    n_dr = rp_ref.shape[1]
    for hh in range(HEADS_PER_BLOCK):
        for dr in range(n_dr):
            base = jnp.broadcast_to(rp_ref[hh, dr:dr + 1, :] * LOG2_E, shape)
            t_lo = pltpu.roll(base, 0, 1, stride=1, stride_axis=0)
            both = jnp.where(low, t_lo, pltpu.roll(t_lo, half, 1))
            xm_ref[hh, dr] = jnp.where(in_window, both, neg)
    for v, (_, tab) in enumerate(plans):
        for hh in range(HEADS_PER_BLOCK):
            for ri in range(QROWS):
                r_lo = (hh * QROWS + ri) * GRID_W
                for j in range(tab.shape[1] // 2):
                    da, db = int(tab[ri, 2 * j]), int(tab[ri, 2 * j + 1])
                    ta = xm_ref[hh, da] if da >= 0 else neg
                    tb = xm_ref[hh, db] if db >= 0 else neg
                    bias_ref[v, r_lo:r_lo + GRID_W, j * LANES:(j + 1) * LANES] = jnp.where(low, ta, tb)


SOFTMAX_SLAB = 16
PIPE_SLOTS = 2


def _softmax_rows(s_ref, n_rows, n_cols, p_ref, mx_ref):
    slabs = [slice(r, r + SOFTMAX_SLAB) for r in range(0, n_rows, SOFTMAX_SLAB)]
    cols = [slice(c, c + LANES) for c in range(0, n_cols, LANES)]
    for rs in slabs:
        top = functools.reduce(jnp.maximum, [s_ref[rs, cs] for cs in cols])
        mx_ref[rs, :] = jnp.max(top, axis=-1, keepdims=True)
    for rs in slabs:
        mx = mx_ref[rs, :]
        for cs in cols:
            p_ref[rs, cs] = jnp.exp2(s_ref[rs, cs] - mx).astype(BF16)


def _pv_heads(p_ref, n_q, pieces):
    half = LANES // 2
    outs = []
    for hh in range(HEADS_PER_BLOCK):
        hs = slice(hh * n_q, (hh + 1) * n_q)
        o = None
        for cols, v in pieces:
            own = (lax.broadcasted_iota(jnp.int32, v.shape, 1) < half) == (hh == 0)
            part = jnp.dot(p_ref[hs, cols], jnp.where(own, v, jnp.ones_like(v)), preferred_element_type=F32)
            o = part if o is None else o + part
        outs.append(o)
    low = lax.broadcasted_iota(jnp.int32, (n_q, LANES), 1) < half
    num = jnp.where(low, outs[0], outs[1])
    den = pltpu.roll(jnp.where(low, outs[1], outs[0]), half, 1)
    return num / den


def _attn_kernel(q_ref, k_ref, v_ref, qc_ref, kc_ref, vc_ref, rp_ref, *rest, rows, kh, need_ctx):
    o_ref = rest[0]
    oc_ref = rest[1] if need_ctx else None
    xm_ref, bias_ref = rest[-2 - 3 * PIPE_SLOTS:-3 * PIPE_SLOTS]
    s_refs = rest[-3 * PIPE_SLOTS:-2 * PIPE_SLOTS]
    p_refs = rest[-2 * PIPE_SLOTS:-PIPE_SLOTS]
    mx_refs = rest[-PIPE_SLOTS:]
    band, plans = _band_plan(rows, kh)
    nblk = rows // QROWS
    qn = QROWS * GRID_W
    kn = band * GRID_W
    lc = kc_ref.shape[0]

    @pl.when(pl.program_id(1) == 0)
    def _():
        _build_bias(rp_ref, xm_ref, bias_ref, plans)

    def offsets(t):
        r0 = t * QROWS
        ub = jnp.clip(r0 - kh // 2, 0, rows - band)
        return pl.multiple_of(r0 * GRID_W, qn), pl.multiple_of(ub * GRID_W, GRID_W)

    def scores(t, s_ref):
        qoff, koff = offsets(t)
        variant = jnp.where(t > 0, 1, 0) + jnp.where(t == nblk - 1, 1, 0)
        qm = _stack_heads(q_ref[pl.ds(qoff, qn), :])
        s_ref[:, :kn] = (lax.dot_general(qm, k_ref[pl.ds(koff, kn), :], _NT, preferred_element_type=F32)
                         + bias_ref[variant])
        s_ref[:, kn:] = lax.dot_general(qm, kc_ref[...], _NT, preferred_element_type=F32)

    def softmax(s_ref, p_ref, mx_ref):
        _softmax_rows(s_ref, HEADS_PER_BLOCK * qn, kn + lc, p_ref, mx_ref)

    def values(t, p_ref):
        qoff, koff = offsets(t)
        pieces = [(slice(0, kn), v_ref[pl.ds(koff, kn), :]), (slice(kn, kn + lc), vc_ref[...])]
        o_ref[pl.ds(qoff, qn), :] = _pv_heads(p_ref, qn, pieces).astype(BF16)

    scores(0, s_refs[0])
    p_refs[1][...] = jnp.ones(p_refs[1].shape, BF16)

    def step(tt, carry):
        a = tt * PIPE_SLOTS
        scores(a + 1, s_refs[1])
        softmax(s_refs[0], p_refs[0], mx_refs[0])
        values(jnp.maximum(a - 1, 0), p_refs[1])
        scores(jnp.minimum(a + 2, nblk - 1), s_refs[0])
        softmax(s_refs[1], p_refs[1], mx_refs[1])
        values(a, p_refs[0])
        return carry

    lax.fori_loop(0, nblk // PIPE_SLOTS, step, 0)
    values(nblk - 1, p_refs[1])

    if need_ctx:
        s_refs[0][:HEADS_PER_BLOCK * lc, :lc] = lax.dot_general(
            _stack_heads(qc_ref[...]), kc_ref[...], _NT, preferred_element_type=F32)
        _softmax_rows(s_refs[0], HEADS_PER_BLOCK * lc, lc, p_refs[0], mx_refs[0])
        oc_ref[...] = _pv_heads(p_refs[0], lc, [(slice(0, lc), vc_ref[...])]).astype(BF16)


def _rpb_rows(rpb):
    h, n_dr, _ = rpb.shape
    gap = jnp.zeros((h, n_dr, LANES - (2 * WIN_W - 1)), F32)
    return jnp.concatenate([rpb[..., WIN_W - 1:].astype(F32), gap, rpb[..., :WIN_W - 1].astype(F32)], axis=-1)


def _attention(qkv_x, qkv_c, rp, batch, need_ctx):
    tx, d3 = qkv_x.shape
    d = d3 // 3
    n = tx // batch
    l = qkv_c.shape[0] // batch
    rows = n // GRID_W
    kh = min(WIN_H, rows)
    npair = d // LANES
    band, plans = _band_plan(rows, kh)
    n_dr = rp.shape[1]
    kern = functools.partial(_attn_kernel, rows=rows, kh=kh, need_ctx=need_ctx)
    in_specs = [
        pl.BlockSpec((n, LANES), lambda p, b: (b, p)),
        pl.BlockSpec((n, LANES), lambda p, b: (b, npair + p)),
        pl.BlockSpec((n, LANES), lambda p, b: (b, 2 * npair + p)),
        pl.BlockSpec((l, LANES), lambda p, b: (b, p)),
        pl.BlockSpec((l, LANES), lambda p, b: (b, npair + p)),
        pl.BlockSpec((l, LANES), lambda p, b: (b, 2 * npair + p)),
        pl.BlockSpec((HEADS_PER_BLOCK, n_dr, LANES), lambda p, b: (p, 0, 0)),
    ]
    out_shape = [jax.ShapeDtypeStruct((tx, d), BF16)]
    out_specs = [pl.BlockSpec((n, LANES), lambda p, b: (b, p))]
    if need_ctx:
        out_shape.append(jax.ShapeDtypeStruct((batch * l, d), BF16))
        out_specs.append(pl.BlockSpec((l, LANES), lambda p, b: (b, p)))
    qrows = HEADS_PER_BLOCK * QROWS * GRID_W
    assert (rows // QROWS) % PIPE_SLOTS == 0 and HEADS_PER_BLOCK * l <= qrows
    bias_bytes = len(plans) * qrows * band * GRID_W * 4
    est = (2 * (4 * n * LANES * 2 + 4 * l * LANES * 2) + bias_bytes
           + HEADS_PER_BLOCK * n_dr * GRID_W * LANES * 4 + 6 * qrows * (band * GRID_W + l) * 4)
    outs = pl.pallas_call(
        kern,
        grid=(npair, batch),
        in_specs=in_specs,
        out_specs=out_specs,
        out_shape=out_shape,
        scratch_shapes=[pltpu.VMEM((HEADS_PER_BLOCK, n_dr, GRID_W, LANES), F32),
                        pltpu.VMEM((len(plans), qrows, band * GRID_W), F32)]
        + [pltpu.VMEM((qrows, band * GRID_W + l), F32)] * PIPE_SLOTS
        + [pltpu.VMEM((qrows, band * GRID_W + l), BF16)] * PIPE_SLOTS
        + [pltpu.VMEM((qrows, 1), F32)] * PIPE_SLOTS,
        compiler_params=_params(est, 2),
        name="nbr_attention",
    )(qkv_x, qkv_x, qkv_x, qkv_c, qkv_c, qkv_c, rp)
    return (outs[0], outs[1]) if need_ctx else (outs[0], None)


def _proj_ffn_kernel(x_ref, a_ref, mod_ref, wo_ref, g_ref, w1_ref, w3_ref, w2_ref, *rest,
                     tm, tpg, gbase, fc, final):
    final_g_ref = rest[0] if final else None
    o_ref, hid_ref = rest[-2:]
    d = x_ref.shape[-1]
    m = _mod_row(mod_ref, tm, tpg, gbase)
    y = jnp.dot(a_ref[...], wo_ref[...], preferred_element_type=F32)
    x1 = x_ref[...] + _chunk(m, 2, d) * y
    _finish(_ffn_tail(x1, m, g_ref[...], w1_ref, w3_ref, w2_ref, hid_ref, fc), final_g_ref, o_ref)


def _conv_ffn_kernel(x_ref, z_ref, zp_ref, zn_ref, gb_ref, cw_ref, mod_ref, wo_ref, g_ref,
                     w1_ref, w3_ref, w2_ref, o_ref, hid_ref, *, tm, tpg, gbase, fc, seq):
    d = x_ref.shape[-1]
    i = pl.program_id(0)
    m = _mod_row(mod_ref, tm, tpg, gbase)
    pos0 = lax.rem(i * tm, seq)
    z = z_ref[...].astype(F32)
    prev = zp_ref[...].astype(F32)[BF16_SUBLANES - 1:, :]
    nxt = zn_ref[...].astype(F32)[:1, :]
    prev = jnp.where(pos0 == 0, jnp.zeros_like(prev), prev)
    nxt = jnp.where(pos0 + tm == seq, jnp.zeros_like(nxt), nxt)
    row = lax.broadcasted_iota(jnp.int32, z.shape, 0)
    z_m1 = jnp.where(row == 0, prev, pltpu.roll(z, 1, 0))
    z_p1 = jnp.where(row == tm - 1, nxt, pltpu.roll(z, tm - 1, 0))
    cw = cw_ref[...]
    conv = z_m1 * cw[0:1] + z * cw[1:2] + z_p1 * cw[2:3]
    a = (gb_ref[...].astype(F32) * conv).astype(BF16)
    y = jnp.dot(a, wo_ref[...], preferred_element_type=F32)
    x1 = x_ref[...] + _chunk(m, 2, d) * y
    _finish(_ffn_tail(x1, m, g_ref[...], w1_ref, w3_ref, w2_ref, hid_ref, fc), None, o_ref)


def _pool_ffn_kernel(x_ref, xp_ref, xn_ref, mod_ref, gm_ref, pw_ref, ps_ref, g_ref,
                     w1_ref, w3_ref, w2_ref, o_ref, hid_ref, hext_ref, *, tm, tpg, gbase, fc, seq):
    d = x_ref.shape[-1]
    halo = F32_SUBLANES
    i = pl.program_id(0)
    m = _mod_row(mod_ref, tm, tpg, gbase)
    shift, scale = _chunk(m, 0, d), _chunk(m, 1, d)
    gm = gm_ref[...]
    pos0 = lax.rem(i * tm, seq)
    x = x_ref[...]
    h_top = _norm_mod(xp_ref[...], gm, shift, scale)
    h_bot = _norm_mod(xn_ref[...], gm, shift, scale)
    hext_ref[0:halo, :] = jnp.where(pos0 == 0, jnp.zeros_like(h_top), h_top)
    hext_ref[halo:halo + tm, :] = _norm_mod(x, gm, shift, scale)
    hext_ref[halo + tm:, :] = jnp.where(pos0 + tm == seq, jnp.zeros_like(h_bot), h_bot)

    pos = pos0 + lax.broadcasted_iota(jnp.int32, (tm, 1), 0)
    gc = d // len(POOL_WINDOWS)
    pieces = []
    for gi, win in enumerate(POOL_WINDOWS):
        cols = slice(gi * gc, (gi + 1) * gc)
        acc = hext_ref[pl.ds(halo - win // 2, tm), cols]
        for dd in range(-win // 2 + 1, win // 2):
            acc = acc + hext_ref[pl.ds(halo + dd, tm), cols]
        cnt = (jnp.minimum(pos + win // 2, seq) - jnp.maximum(pos - win // 2, 0)).astype(F32)
        pooled = (acc / cnt - hext_ref[pl.ds(halo, tm), cols]).astype(BF16)
        pieces.append(jnp.dot(pooled, pw_ref[gi], preferred_element_type=F32))
    y = jnp.concatenate(pieces, axis=-1) * ps_ref[...]
    x1 = x + _chunk(m, 2, d) * y
    _finish(_ffn_tail(x1, m, g_ref[...], w1_ref, w3_ref, w2_ref, hid_ref, fc), None, o_ref)


def _ffn_est(tm, d, f):
    return 3 * d * f * 2 + 4 * tm * d * 4 + tm * f * 2 + 6 * tm * d * 4 + 3 * tm * 512 * 4


def _ffn_consts(g_ffn, w1, w3, w2):
    return ([g_ffn, w1, w3, w2],
            [_const_spec(g_ffn.shape), _const_spec(w1.shape), _const_spec(w3.shape), _const_spec(w2.shape)])


def _proj_ffn(x, a, mod, wo, ffn, tm, tpg, gbase, fc, final_g=None):
    t, d = x.shape
    f = ffn[1].shape[1]
    tile = pl.BlockSpec((tm, d), lambda i: (i, 0))
    fargs, fspecs = _ffn_consts(*ffn)
    args = [x, a, mod, wo] + fargs
    specs = [tile, tile, _const_spec(mod.shape), _const_spec(wo.shape)] + fspecs
    if final_g is not None:
        args.append(final_g)
        specs.append(_const_spec(final_g.shape))
    kern = functools.partial(_proj_ffn_kernel, tm=tm, tpg=tpg, gbase=gbase, fc=fc, final=final_g is not None)
    return pl.pallas_call(
        kern, grid=(t // tm,), in_specs=specs, out_specs=tile,
        out_shape=jax.ShapeDtypeStruct((t, d), F32),
        scratch_shapes=[pltpu.VMEM((tm, f), BF16)],
        compiler_params=_params(_ffn_est(tm, d, f) + d * d * 2 + 2 * tm * d * 2),
        name="proj_ffn",
    )(*args)


def _conv_ffn(x, z, gb, conv_w, mod, wo, ffn, tm, tpg, gbase, fc, seq):
    t, d = x.shape
    f = ffn[1].shape[1]
    tile = pl.BlockSpec((tm, d), lambda i: (i, 0))
    hb = tm // BF16_SUBLANES
    nhb = t // BF16_SUBLANES
    prev = pl.BlockSpec((BF16_SUBLANES, d), lambda i: (jnp.maximum(i * hb - 1, 0), 0))
    nxt = pl.BlockSpec((BF16_SUBLANES, d), lambda i: (jnp.minimum((i + 1) * hb, nhb - 1), 0))
    fargs, fspecs = _ffn_consts(*ffn)
    kern = functools.partial(_conv_ffn_kernel, tm=tm, tpg=tpg, gbase=gbase, fc=fc, seq=seq)
    return pl.pallas_call(
        kern, grid=(t // tm,),
        in_specs=[tile, tile, prev, nxt, tile, _const_spec(conv_w.shape), _const_spec(mod.shape),
                  _const_spec(wo.shape)] + fspecs,
        out_specs=tile,
        out_shape=jax.ShapeDtypeStruct((t, d), F32),
        scratch_shapes=[pltpu.VMEM((tm, f), BF16)],
        compiler_params=_params(_ffn_est(tm, d, f) + d * d * 2 + 4 * tm * d * 2 + 4 * tm * d * 4),
        name="conv_ffn",
    )(x, z, z, z, gb, conv_w, mod, wo, *fargs)


def _pool_ffn(x, mod, g_mix, pool_w, pool_scale, ffn, tm, tpg, gbase, fc, seq):
    t, d = x.shape
    f = ffn[1].shape[1]
    tile = pl.BlockSpec((tm, d), lambda i: (i, 0))
    hb = tm // F32_SUBLANES
    nhb = t // F32_SUBLANES
    prev = pl.BlockSpec((F32_SUBLANES, d), lambda i: (jnp.maximum(i * hb - 1, 0), 0))
    nxt = pl.BlockSpec((F32_SUBLANES, d), lambda i: (jnp.minimum((i + 1) * hb, nhb - 1), 0))
    fargs, fspecs = _ffn_consts(*ffn)
    kern = functools.partial(_pool_ffn_kernel, tm=tm, tpg=tpg, gbase=gbase, fc=fc, seq=seq)
    return pl.pallas_call(
        kern, grid=(t // tm,),
        in_specs=[tile, prev, nxt, _const_spec(mod.shape), _const_spec(g_mix.shape),
                  _const_spec(pool_w.shape), _const_spec(pool_scale.shape)] + fspecs,
        out_specs=tile,
        out_shape=jax.ShapeDtypeStruct((t, d), F32),
        scratch_shapes=[pltpu.VMEM((tm, f), BF16), pltpu.VMEM((tm + 2 * F32_SUBLANES, d), F32)],
        compiler_params=_params(_ffn_est(tm, d, f) + 3 * tm * d * 4),
        name="pool_ffn",
    )(x, x, x, mod, g_mix, pool_w, pool_scale, *fargs)


def _token_tile(seq):
    return min(seq, 512)


def kernel(x, c, ctx, c_ctx, adaln_w, adaln_b, norm_mix_g, norm_ffn_g, ffn_w1, ffn_w3, ffn_w2,
           na_w_qkv, na_w_o, na_rpb, sc_w_in, sc_conv_w, sc_w_out, pool_w, pool_scale, final_g):
    batch, n, d = x.shape
    l = ctx.shape[1]
    depth = adaln_w.shape[0]
    f = ffn_w1.shape[-1]
    assert batch + 1 <= MOD_ROWS and d % LANES == 0 and n % GRID_W == 0
    assert d // N_HEADS * HEADS_PER_BLOCK == LANES
    fc = 256
    assert f % fc == 0
    assert (depth - 1) % N_MIXERS == 0, "the final norm is fused into the attention-layer tail"
    tm_x, tm_c = _token_tile(n), _token_tile(l)
    q_scale = float(d // N_HEADS) ** -0.5 * LOG2_E

    c_rows = jnp.concatenate([c, c_ctx[None, :], jnp.zeros((MOD_ROWS - batch - 1, d), F32)], axis=0)
    mod_all = _modulation(c_rows, adaln_w, adaln_b)

    xs = x.reshape(batch * n, d)
    cs = ctx.reshape(batch * l, d)
    x_grp = dict(tm=tm_x, tpg=n, gbase=0)
    c_grp = dict(tm=tm_c, tpg=batch * l, gbase=batch)

    for layer in range(depth):
        kind = layer % N_MIXERS
        j = layer // N_MIXERS
        need_ctx = layer < depth - 1
        mod = mod_all[layer]
        g_mix = norm_mix_g[layer].reshape(1, d)
        ffn = (norm_ffn_g[layer].reshape(1, d), ffn_w1[layer].astype(BF16), ffn_w3[layer].astype(BF16),
               ffn_w2[layer].astype(BF16))
        fin = final_g.reshape(1, d) if layer == depth - 1 else None

        if kind == 0:
            wqkv = na_w_qkv[j].astype(BF16)
            wo = na_w_o[j].astype(BF16)
            rp = _rpb_rows(na_rpb[j])
            qkv_x = _qkv_proj(xs, mod, g_mix, wqkv, q_scale=q_scale, **x_grp)
            qkv_c = _qkv_proj(cs, mod, g_mix, wqkv, q_scale=q_scale, **c_grp)
            o_x, o_c = _attention(qkv_x, qkv_c, rp, batch, need_ctx)
            xs = _proj_ffn(xs, o_x, mod, wo, ffn, fc=fc, final_g=fin, **x_grp)
            if need_ctx:
                cs = _proj_ffn(cs, o_c, mod, wo, ffn, fc=fc, **c_grp)
        elif kind == 1:
            w_in = sc_w_in[j].astype(BF16)
            wo = sc_w_out[j].astype(BF16)
            gb_x, z_x = _conv_in_proj(xs, mod, g_mix, w_in, **x_grp)
            xs_new = _conv_ffn(xs, z_x, gb_x, sc_conv_w[j], mod, wo, ffn, fc=fc, seq=n, **x_grp)
            if need_ctx:
                gb_c, z_c = _conv_in_proj(cs, mod, g_mix, w_in, **c_grp)
                cs = _conv_ffn(cs, z_c, gb_c, sc_conv_w[j], mod, wo, ffn, fc=fc, seq=l, **c_grp)
            xs = xs_new
        else:
            pw = pool_w[j].astype(BF16)
            ps = pool_scale[j].reshape(1, d)
            xs_new = _pool_ffn(xs, mod, g_mix, pw, ps, ffn, fc=fc, seq=n, **x_grp)
            if need_ctx:
                cs = _pool_ffn(cs, mod, g_mix, pw, ps, ffn, fc=fc, seq=l, **c_grp)
            xs = xs_new
    return xs.reshape(batch, n, d)
```

```python
import functools

import numpy as np
import jax
import jax.numpy as jnp
from jax import lax
from jax.experimental import pallas as pl
from jax.experimental.pallas import tpu as pltpu

F32 = jnp.float32
BF16 = jnp.bfloat16

GRID_W = 64
N_MIXERS = 3
N_HEADS = 16
WIN_H = 8
WIN_W = 16
CONV_W = 3
POOL_WINDOWS = (2, 4, 8, 16)
N_MOD = 6
EPS = 1e-6
LOG2_E = 1.4426950408889634

V7X_VMEM_BYTES = 64 * 1024 * 1024
LANES = 128
F32_SUBLANES = 8
BF16_SUBLANES = 16

HEADS_PER_BLOCK = 2
MOD_ROWS = 8


def _vmem_limit(est_bytes):
    return int(min(V7X_VMEM_BYTES - 6 * 1024 * 1024, max(32 * 1024 * 1024, est_bytes * 5 // 4)))


def _params(est_bytes, ndims=1):
    return pltpu.CompilerParams(dimension_semantics=("arbitrary",) * ndims,
                                vmem_limit_bytes=_vmem_limit(est_bytes))


def _const_spec(shape):
    nd = len(shape)
    return pl.BlockSpec(shape, lambda *_: (0,) * nd, pipeline_mode=pl.Buffered(1))


def _layer_spec(stack, idx):
    nd = stack.ndim
    return pl.BlockSpec((None,) + stack.shape[1:], lambda *_: (idx,) + (0,) * (nd - 1),
                        pipeline_mode=pl.Buffered(1))


def _split_bf16(v):
    hi = v.astype(BF16)
    lo = (v - hi.astype(F32)).astype(BF16)
    return hi, lo


def _mod_kernel(c_ref, w_ref, b_ref, o_ref):
    c = c_ref[...]
    s = c * jax.nn.sigmoid(c)
    s_hi, s_lo = _split_bf16(s)
    lhs = jnp.concatenate([s_hi, s_lo], axis=0)
    w_hi, w_lo = _split_bf16(w_ref[...])
    r = (jnp.dot(lhs, w_hi, preferred_element_type=F32)
         + jnp.dot(lhs, w_lo, preferred_element_type=F32))
    o_ref[...] = r[:MOD_ROWS] + r[MOD_ROWS:] + b_ref[...]


def _modulation(c_rows, adaln_w, adaln_b):
    depth, d, nm = adaln_w.shape
    nc = nm // 4
    est = 2 * d * nc * 4 + 3 * d * nc * 2 + 4 * MOD_ROWS * nc * 4
    return pl.pallas_call(
        _mod_kernel,
        grid=(depth, nm // nc),
        in_specs=[
            pl.BlockSpec((MOD_ROWS, d), lambda l, j: (0, 0)),
            pl.BlockSpec((None, d, nc), lambda l, j: (l, 0, j)),
            pl.BlockSpec((None, 1, nc), lambda l, j: (l, 0, j)),
        ],
        out_specs=pl.BlockSpec((None, MOD_ROWS, nc), lambda l, j: (l, 0, j)),
        out_shape=jax.ShapeDtypeStruct((depth, MOD_ROWS, nm), F32),
        compiler_params=_params(est, 2),
        name="adaln_mod",
    )(c_rows, adaln_w, adaln_b.reshape(depth, 1, nm))


def _mod_row(mod_ref, tm, tokens_per_group, group_base):
    grp = group_base + lax.div(pl.program_id(0) * tm, tokens_per_group)
    return mod_ref[pl.ds(grp, 1), :]


def _chunk(m, idx, d):
    return m[:, idx * d:(idx + 1) * d]


def _rms(x, g):
    ms = jnp.mean(x * x, axis=-1, keepdims=True)
    return x * lax.rsqrt(ms + EPS) * g


def _norm_mod(x, g, shift, scale):
    return _rms(x, g) * (1.0 + scale) + shift


def _ffn_tail(x1, m, g_ffn, w1_ref, w3_ref, w2_ref, hid_ref, fc):
    d = x1.shape[-1]
    f = w1_ref.shape[-1]
    h = _norm_mod(x1, g_ffn, _chunk(m, 3, d), _chunk(m, 4, d)).astype(BF16)
    for c in range(f // fc):
        sl = slice(c * fc, (c + 1) * fc)
        a = jnp.dot(h, w1_ref[:, sl], preferred_element_type=F32)
        b = jnp.dot(h, w3_ref[:, sl], preferred_element_type=F32)
        hid_ref[:, sl] = (a * jax.nn.sigmoid(a) * b).astype(BF16)
    y = jnp.dot(hid_ref[...], w2_ref[...], preferred_element_type=F32)
    return x1 + _chunk(m, 5, d) * y


def _finish(out, final_g_ref, o_ref):
    if final_g_ref is not None:
        out = _rms(out, final_g_ref[...])
    o_ref[...] = out


def _qkv_kernel(x_ref, mod_ref, g_ref, w_ref, o_ref, *, tm, tpg, gbase, q_scale):
    d = x_ref.shape[-1]
    m = _mod_row(mod_ref, tm, tpg, gbase)
    h = _norm_mod(x_ref[...], g_ref[...], _chunk(m, 0, d), _chunk(m, 1, d)).astype(BF16)
    for j in range(3):
        r = jnp.dot(h, w_ref[:, j * d:(j + 1) * d], preferred_element_type=F32)
        if j == 0:
            r = r * q_scale
        o_ref[:, j * d:(j + 1) * d] = r.astype(BF16)


def _conv_in_kernel(x_ref, mod_ref, g_ref, w_ref, gb_ref, z_ref, *, tm, tpg, gbase):
    d = x_ref.shape[-1]
    m = _mod_row(mod_ref, tm, tpg, gbase)
    h = _norm_mod(x_ref[...], g_ref[...], _chunk(m, 0, d), _chunk(m, 1, d)).astype(BF16)
    gb_ref[...] = jnp.dot(h, w_ref[:, :d], preferred_element_type=F32).astype(BF16)
    gc = jnp.dot(h, w_ref[:, d:2 * d], preferred_element_type=F32)
    val = jnp.dot(h, w_ref[:, 2 * d:], preferred_element_type=F32)
    z_ref[...] = (gc * val).astype(BF16)


def _front_call(kernel, x, mod, g, w, out_shapes, out_specs, tm, name):
    t, d = x.shape
    w, wi = w
    nout = w.shape[-1]
    est = 2 * tm * d * 4 + d * nout * 2 + 2 * tm * nout * 2 + tm * nout * 4 + 4 * tm * d * 4
    return pl.pallas_call(
        kernel,
        grid=(t // tm,),
        in_specs=[
            pl.BlockSpec((tm, d), lambda i: (i, 0)),
            _const_spec(mod.shape),
            _const_spec(g.shape),
            _layer_spec(w, wi),
        ],
        out_specs=out_specs,
        out_shape=out_shapes,
        compiler_params=_params(est),
        name=name,
    )(x, mod, g, w)


def _qkv_proj(x, mod, g, w, tm, tpg, gbase, q_scale):
    t, d = x.shape
    kern = functools.partial(_qkv_kernel, tm=tm, tpg=tpg, gbase=gbase, q_scale=q_scale)
    return _front_call(kern, x, mod, g, w,
                       jax.ShapeDtypeStruct((t, 3 * d), BF16),
                       pl.BlockSpec((tm, 3 * d), lambda i: (i, 0)), tm, "qkv_proj")


def _conv_in_proj(x, mod, g, w, tm, tpg, gbase):
    t, d = x.shape
    kern = functools.partial(_conv_in_kernel, tm=tm, tpg=tpg, gbase=gbase)
    spec = pl.BlockSpec((tm, d), lambda i: (i, 0))
    return _front_call(kern, x, mod, g, w,
                       (jax.ShapeDtypeStruct((t, d), BF16), jax.ShapeDtypeStruct((t, d), BF16)),
                       (spec, spec), tm, "conv_in_proj")


def _stack_heads(q):
    lane = lax.broadcasted_iota(jnp.int32, q.shape, 1)
    zero = jnp.zeros_like(q)
    return jnp.concatenate([jnp.where(lane < LANES // 2, q, zero),
                            jnp.where(lane >= LANES // 2, q, zero)], axis=0)


_NT = (((1,), (1,)), ((), ()))


QROWS = 4


def _band_plan(rows, kh):
    band = QROWS + kh
    assert rows % QROWS == 0 and band % 2 == 0 and rows >= band + QROWS

    def plan(r0):
        ub = int(np.clip(r0 - kh // 2, 0, rows - band))
        tab = np.full((QROWS, band), -1, np.int64)
        for ri in range(QROWS):
            r = r0 + ri
            start = int(np.clip(r - kh // 2, 0, rows - kh))
            for u in range(band):
                if start <= ub + u < start + kh:
                    tab[ri, u] = ub + u - r + WIN_H - 1
        return ub - r0, tab

    first, last = plan(0), plan(rows - QROWS)
    interior = plan(QROWS)
    for r0 in range(QROWS, rows - QROWS, QROWS):
        off, tab = plan(r0)
        assert off == interior[0] and (tab == interior[1]).all()
    return band, [first, interior, last]


def _build_bias(rp_ref, xm_ref, bias_ref, plans):
    half = LANES // 2
    shape = (GRID_W, LANES)
    qcol = lax.broadcasted_iota(jnp.int32, shape, 0)
    lane = lax.broadcasted_iota(jnp.int32, shape, 1)
    kcol = jnp.bitwise_and(lane, half - 1)
    col_start = jnp.clip(qcol - WIN_W // 2, 0, GRID_W - WIN_W)
    in_window = (kcol >= col_start) & (kcol < col_start + WIN_W)
    low = lane < half
    neg = jnp.full(shape, -jnp.inf, F32)
    n_dr = rp_ref.shape[1]
    for hh in range(HEADS_PER_BLOCK):
        for dr in range(n_dr):
            base = jnp.broadcast_to(rp_ref[hh, dr:dr + 1, :] * LOG2_E, shape)
            t_lo = pltpu.roll(base, 0, 1, stride=1, stride_axis=0)
            both = jnp.where(low, t_lo, pltpu.roll(t_lo, half, 1))
            xm_ref[hh, dr] = jnp.where(in_window, both, neg)
    for v, (_, tab) in enumerate(plans):
        for hh in range(HEADS_PER_BLOCK):
            for ri in range(QROWS):
                r_lo = (hh * QROWS + ri) * GRID_W
                for j in range(tab.shape[1] // 2):
                    da, db = int(tab[ri, 2 * j]), int(tab[ri, 2 * j + 1])
                    ta = xm_ref[hh, da] if da >= 0 else neg
                    tb = xm_ref[hh, db] if db >= 0 else neg
                    bias_ref[v, r_lo:r_lo + GRID_W, j * LANES:(j + 1) * LANES] = jnp.where(low, ta, tb)


SOFTMAX_SLAB = 16
BLOCKS_PER_STEP = 8


def _softmax_rows(s_parts, p_ref):
    n_rows = s_parts[0].shape[0]
    for r in range(0, n_rows, SOFTMAX_SLAB):
        rs = slice(r, r + SOFTMAX_SLAB)
        blocks = [s[rs, c:c + LANES] for s in s_parts for c in range(0, s.shape[1], LANES)]
        mx = jnp.max(functools.reduce(jnp.maximum, blocks), axis=-1, keepdims=True)
        for j, blk in enumerate(blocks):
            p_ref[rs, j * LANES:(j + 1) * LANES] = jnp.exp2(blk - mx).astype(BF16)


def _pv_heads(p_ref, n_q, pieces):
    half = LANES // 2
    outs = []
    for hh in range(HEADS_PER_BLOCK):
        hs = slice(hh * n_q, (hh + 1) * n_q)
        o = None
        for cols, v in pieces:
            own = (lax.broadcasted_iota(jnp.int32, v.shape, 1) < half) == (hh == 0)
            part = jnp.dot(p_ref[hs, cols], jnp.where(own, v, jnp.ones_like(v)), preferred_element_type=F32)
            o = part if o is None else o + part
        outs.append(o)
    low = lax.broadcasted_iota(jnp.int32, (n_q, LANES), 1) < half
    num = jnp.where(low, outs[0], outs[1])
    den = pltpu.roll(jnp.where(low, outs[1], outs[0]), half, 1)
    return num / den


def _attn_kernel(q_ref, k_ref, v_ref, qc_ref, kc_ref, vc_ref, rp_ref, *rest, rows, kh, need_ctx):
    o_ref = rest[0]
    oc_ref = rest[1] if need_ctx else None
    xm_ref, bias_ref = rest[-2 - BLOCKS_PER_STEP:-BLOCKS_PER_STEP]
    p_refs = rest[-BLOCKS_PER_STEP:]
    band, plans = _band_plan(rows, kh)
    nblk = rows // QROWS
    qn = QROWS * GRID_W
    kn = band * GRID_W
    lc = kc_ref.shape[0]

    @pl.when(pl.program_id(1) == 0)
    def _():
        _build_bias(rp_ref, xm_ref, bias_ref, plans)

    def scores(t):
        r0 = t * QROWS
        ub = jnp.clip(r0 - kh // 2, 0, rows - band)
        variant = jnp.where(t > 0, 1, 0) + jnp.where(t == nblk - 1, 1, 0)
        qoff = pl.multiple_of(r0 * GRID_W, qn)
        koff = pl.multiple_of(ub * GRID_W, GRID_W)
        qm = _stack_heads(q_ref[pl.ds(qoff, qn), :])
        s_lat = (lax.dot_general(qm, k_ref[pl.ds(koff, kn), :], _NT, preferred_element_type=F32)
                 + bias_ref[variant])
        s_ctx = lax.dot_general(qm, kc_ref[...], _NT, preferred_element_type=F32)
        return qoff, koff, s_lat, s_ctx

    def finish(blk, p_ref):
        qoff, koff, s_lat, s_ctx = blk
        _softmax_rows([s_lat, s_ctx], p_ref)
        pieces = [(slice(0, kn), v_ref[pl.ds(koff, kn), :]), (slice(kn, kn + lc), vc_ref[...])]
        o_ref[pl.ds(qoff, qn), :] = _pv_heads(p_ref, qn, pieces).astype(BF16)

    def step(tt, carry):
        blocks = [scores(tt * BLOCKS_PER_STEP + i) for i in range(BLOCKS_PER_STEP)]
        for blk, p_ref in zip(blocks, p_refs):
            finish(blk, p_ref)
        return carry

    lax.fori_loop(0, nblk // BLOCKS_PER_STEP, step, 0)

    if need_ctx:
        s = lax.dot_general(_stack_heads(qc_ref[...]), kc_ref[...], _NT, preferred_element_type=F32)
        _softmax_rows([s], p_refs[0])
        oc_ref[...] = _pv_heads(p_refs[0], lc, [(slice(0, lc), vc_ref[...])]).astype(BF16)


def _rpb_rows(rpb):
    h, n_dr, _ = rpb.shape
    gap = jnp.zeros((h, n_dr, LANES - (2 * WIN_W - 1)), F32)
    return jnp.concatenate([rpb[..., WIN_W - 1:].astype(F32), gap, rpb[..., :WIN_W - 1].astype(F32)], axis=-1)


def _attention(qkv_x, qkv_c, rp, batch, need_ctx):
    tx, d3 = qkv_x.shape
    d = d3 // 3
    n = tx // batch
    l = qkv_c.shape[0] // batch
    rows = n // GRID_W
    kh = min(WIN_H, rows)
    npair = d // LANES
    band, plans = _band_plan(rows, kh)
    n_dr = rp.shape[1]
    kern = functools.partial(_attn_kernel, rows=rows, kh=kh, need_ctx=need_ctx)
    in_specs = [
        pl.BlockSpec((n, LANES), lambda p, b: (b, p)),
        pl.BlockSpec((n, LANES), lambda p, b: (b, npair + p)),
        pl.BlockSpec((n, LANES), lambda p, b: (b, 2 * npair + p)),
        pl.BlockSpec((l, LANES), lambda p, b: (b, p)),
        pl.BlockSpec((l, LANES), lambda p, b: (b, npair + p)),
        pl.BlockSpec((l, LANES), lambda p, b: (b, 2 * npair + p)),
        pl.BlockSpec((HEADS_PER_BLOCK, n_dr, LANES), lambda p, b: (p, 0, 0)),
    ]
    out_shape = [jax.ShapeDtypeStruct((tx, d), BF16)]
    out_specs = [pl.BlockSpec((n, LANES), lambda p, b: (b, p))]
    if need_ctx:
        out_shape.append(jax.ShapeDtypeStruct((batch * l, d), BF16))
        out_specs.append(pl.BlockSpec((l, LANES), lambda p, b: (b, p)))
    qrows = HEADS_PER_BLOCK * QROWS * GRID_W
    assert (rows // QROWS) % BLOCKS_PER_STEP == 0 and HEADS_PER_BLOCK * l <= qrows
    bias_bytes = len(plans) * qrows * band * GRID_W * 4
    est = (2 * (4 * n * LANES * 2 + 4 * l * LANES * 2) + bias_bytes
           + HEADS_PER_BLOCK * n_dr * GRID_W * LANES * 4 + 6 * qrows * (band * GRID_W + l) * 4)
    outs = pl.pallas_call(
        kern,
        grid=(npair, batch),
        in_specs=in_specs,
        out_specs=out_specs,
        out_shape=out_shape,
        scratch_shapes=[pltpu.VMEM((HEADS_PER_BLOCK, n_dr, GRID_W, LANES), F32),
                        pltpu.VMEM((len(plans), qrows, band * GRID_W), F32)]
        + [pltpu.VMEM((qrows, band * GRID_W + l), BF16)] * BLOCKS_PER_STEP,
        compiler_params=_params(est, 2),
        name="nbr_attention",
    )(qkv_x, qkv_x, qkv_x, qkv_c, qkv_c, qkv_c, rp)
    return (outs[0], outs[1]) if need_ctx else (outs[0], None)


def _proj_ffn_kernel(x_ref, a_ref, mod_ref, wo_ref, g_ref, w1_ref, w3_ref, w2_ref, *rest,
                     tm, tpg, gbase, fc, final):
    final_g_ref = rest[0] if final else None
    o_ref, hid_ref = rest[-2:]
    d = x_ref.shape[-1]
    m = _mod_row(mod_ref, tm, tpg, gbase)
    y = jnp.dot(a_ref[...], wo_ref[...], preferred_element_type=F32)
    x1 = x_ref[...] + _chunk(m, 2, d) * y
    _finish(_ffn_tail(x1, m, g_ref[...], w1_ref, w3_ref, w2_ref, hid_ref, fc), final_g_ref, o_ref)


def _conv_ffn_kernel(x_ref, z_ref, zp_ref, zn_ref, gb_ref, cw_ref, mod_ref, wo_ref, g_ref,
                     w1_ref, w3_ref, w2_ref, o_ref, hid_ref, *, tm, tpg, gbase, fc, seq):
    d = x_ref.shape[-1]
    i = pl.program_id(0)
    m = _mod_row(mod_ref, tm, tpg, gbase)
    pos0 = lax.rem(i * tm, seq)
    z = z_ref[...].astype(F32)
    prev = zp_ref[...].astype(F32)[BF16_SUBLANES - 1:, :]
    nxt = zn_ref[...].astype(F32)[:1, :]
    prev = jnp.where(pos0 == 0, jnp.zeros_like(prev), prev)
    nxt = jnp.where(pos0 + tm == seq, jnp.zeros_like(nxt), nxt)
    row = lax.broadcasted_iota(jnp.int32, z.shape, 0)
    z_m1 = jnp.where(row == 0, prev, pltpu.roll(z, 1, 0))
    z_p1 = jnp.where(row == tm - 1, nxt, pltpu.roll(z, tm - 1, 0))
    cw = cw_ref[...]
    conv = z_m1 * cw[0:1] + z * cw[1:2] + z_p1 * cw[2:3]
    a = (gb_ref[...].astype(F32) * conv).astype(BF16)
    y = jnp.dot(a, wo_ref[...], preferred_element_type=F32)
    x1 = x_ref[...] + _chunk(m, 2, d) * y
    _finish(_ffn_tail(x1, m, g_ref[...], w1_ref, w3_ref, w2_ref, hid_ref, fc), None, o_ref)


def _pool_ffn_kernel(x_ref, xp_ref, xn_ref, mod_ref, gm_ref, pw_ref, ps_ref, g_ref,
                     w1_ref, w3_ref, w2_ref, o_ref, hid_ref, hext_ref, *, tm, tpg, gbase, fc, seq):
    d = x_ref.shape[-1]
    halo = F32_SUBLANES
    i = pl.program_id(0)
    m = _mod_row(mod_ref, tm, tpg, gbase)
    shift, scale = _chunk(m, 0, d), _chunk(m, 1, d)
    gm = gm_ref[...]
    pos0 = lax.rem(i * tm, seq)
    x = x_ref[...]
    h_top = _norm_mod(xp_ref[...], gm, shift, scale)
    h_bot = _norm_mod(xn_ref[...], gm, shift, scale)
    hext_ref[0:halo, :] = jnp.where(pos0 == 0, jnp.zeros_like(h_top), h_top)
    hext_ref[halo:halo + tm, :] = _norm_mod(x, gm, shift, scale)
    hext_ref[halo + tm:, :] = jnp.where(pos0 + tm == seq, jnp.zeros_like(h_bot), h_bot)

    pos = pos0 + lax.broadcasted_iota(jnp.int32, (tm, 1), 0)
    gc = d // len(POOL_WINDOWS)
    pieces = []
    for gi, win in enumerate(POOL_WINDOWS):
        cols = slice(gi * gc, (gi + 1) * gc)
        acc = hext_ref[pl.ds(halo - win // 2, tm), cols]
        for dd in range(-win // 2 + 1, win // 2):
            acc = acc + hext_ref[pl.ds(halo + dd, tm), cols]
        cnt = (jnp.minimum(pos + win // 2, seq) - jnp.maximum(pos - win // 2, 0)).astype(F32)
        pooled = (acc / cnt - hext_ref[pl.ds(halo, tm), cols]).astype(BF16)
        pieces.append(jnp.dot(pooled, pw_ref[gi], preferred_element_type=F32))
    y = jnp.concatenate(pieces, axis=-1) * ps_ref[...]
    x1 = x + _chunk(m, 2, d) * y
    _finish(_ffn_tail(x1, m, g_ref[...], w1_ref, w3_ref, w2_ref, hid_ref, fc), None, o_ref)


def _ffn_est(tm, d, f):
    return 3 * d * f * 2 + 4 * tm * d * 4 + tm * f * 2 + 6 * tm * d * 4 + 3 * tm * 512 * 4


def _ffn_consts(g_ffn, w1, w3, w2, layer):
    return ([g_ffn, w1, w3, w2],
            [_const_spec(g_ffn.shape), _layer_spec(w1, layer), _layer_spec(w3, layer), _layer_spec(w2, layer)])


def _proj_ffn(x, a, mod, wo, ffn, tm, tpg, gbase, fc, final_g=None):
    t, d = x.shape
    f = ffn[1].shape[-1]
    wo, woi = wo
    tile = pl.BlockSpec((tm, d), lambda i: (i, 0))
    fargs, fspecs = _ffn_consts(*ffn)
    args = [x, a, mod, wo] + fargs
    specs = [tile, tile, _const_spec(mod.shape), _layer_spec(wo, woi)] + fspecs
    if final_g is not None:
        args.append(final_g)
        specs.append(_const_spec(final_g.shape))
    kern = functools.partial(_proj_ffn_kernel, tm=tm, tpg=tpg, gbase=gbase, fc=fc, final=final_g is not None)
    return pl.pallas_call(
        kern, grid=(t // tm,), in_specs=specs, out_specs=tile,
        out_shape=jax.ShapeDtypeStruct((t, d), F32),
        scratch_shapes=[pltpu.VMEM((tm, f), BF16)],
        compiler_params=_params(_ffn_est(tm, d, f) + d * d * 2 + 2 * tm * d * 2),
        name="proj_ffn",
    )(*args)


def _conv_ffn(x, z, gb, conv_w, mod, wo, ffn, tm, tpg, gbase, fc, seq):
    t, d = x.shape
    f = ffn[1].shape[-1]
    wo, woi = wo
    tile = pl.BlockSpec((tm, d), lambda i: (i, 0))
    hb = tm // BF16_SUBLANES
    nhb = t // BF16_SUBLANES
    prev = pl.BlockSpec((BF16_SUBLANES, d), lambda i: (jnp.maximum(i * hb - 1, 0), 0))
    nxt = pl.BlockSpec((BF16_SUBLANES, d), lambda i: (jnp.minimum((i + 1) * hb, nhb - 1), 0))
    fargs, fspecs = _ffn_consts(*ffn)
    kern = functools.partial(_conv_ffn_kernel, tm=tm, tpg=tpg, gbase=gbase, fc=fc, seq=seq)
    return pl.pallas_call(
        kern, grid=(t // tm,),
        in_specs=[tile, tile, prev, nxt, tile, _const_spec(conv_w.shape), _const_spec(mod.shape),
                  _layer_spec(wo, woi)] + fspecs,
        out_specs=tile,
        out_shape=jax.ShapeDtypeStruct((t, d), F32),
        scratch_shapes=[pltpu.VMEM((tm, f), BF16)],
        compiler_params=_params(_ffn_est(tm, d, f) + d * d * 2 + 4 * tm * d * 2 + 4 * tm * d * 4),
        name="conv_ffn",
    )(x, z, z, z, gb, conv_w, mod, wo, *fargs)


def _pool_ffn(x, mod, g_mix, pool_w, pool_scale, ffn, tm, tpg, gbase, fc, seq):
    t, d = x.shape
    f = ffn[1].shape[-1]
    pool_w, pwi = pool_w
    tile = pl.BlockSpec((tm, d), lambda i: (i, 0))
    hb = tm // F32_SUBLANES
    nhb = t // F32_SUBLANES
    prev = pl.BlockSpec((F32_SUBLANES, d), lambda i: (jnp.maximum(i * hb - 1, 0), 0))
    nxt = pl.BlockSpec((F32_SUBLANES, d), lambda i: (jnp.minimum((i + 1) * hb, nhb - 1), 0))
    fargs, fspecs = _ffn_consts(*ffn)
    kern = functools.partial(_pool_ffn_kernel, tm=tm, tpg=tpg, gbase=gbase, fc=fc, seq=seq)
    return pl.pallas_call(
        kern, grid=(t // tm,),
        in_specs=[tile, prev, nxt, _const_spec(mod.shape), _const_spec(g_mix.shape),
                  _layer_spec(pool_w, pwi), _const_spec(pool_scale.shape)] + fspecs,
        out_specs=tile,
        out_shape=jax.ShapeDtypeStruct((t, d), F32),
        scratch_shapes=[pltpu.VMEM((tm, f), BF16), pltpu.VMEM((tm + 2 * F32_SUBLANES, d), F32)],
        compiler_params=_params(_ffn_est(tm, d, f) + 3 * tm * d * 4),
        name="pool_ffn",
    )(x, x, x, mod, g_mix, pool_w, pool_scale, *fargs)


def _token_tile(seq):
    return min(seq, 512)


def kernel(x, c, ctx, c_ctx, adaln_w, adaln_b, norm_mix_g, norm_ffn_g, ffn_w1, ffn_w3, ffn_w2,
           na_w_qkv, na_w_o, na_rpb, sc_w_in, sc_conv_w, sc_w_out, pool_w, pool_scale, final_g):
    batch, n, d = x.shape
    l = ctx.shape[1]
    depth = adaln_w.shape[0]
    f = ffn_w1.shape[-1]
    assert batch + 1 <= MOD_ROWS and d % LANES == 0 and n % GRID_W == 0
    assert d // N_HEADS * HEADS_PER_BLOCK == LANES
    fc = 256
    assert f % fc == 0
    assert (depth - 1) % N_MIXERS == 0, "the final norm is fused into the attention-layer tail"
    tm_x, tm_c = _token_tile(n), _token_tile(l)
    q_scale = float(d // N_HEADS) ** -0.5 * LOG2_E

    c_rows = jnp.concatenate([c, c_ctx[None, :], jnp.zeros((MOD_ROWS - batch - 1, d), F32)], axis=0)
    mod_all = _modulation(c_rows, adaln_w, adaln_b)

    w1_all, w3_all, w2_all = ffn_w1.astype(BF16), ffn_w3.astype(BF16), ffn_w2.astype(BF16)
    wqkv_all, wo_all = na_w_qkv.astype(BF16), na_w_o.astype(BF16)
    win_all, wout_all = sc_w_in.astype(BF16), sc_w_out.astype(BF16)
    pw_all = pool_w.astype(BF16)

    xs = x.reshape(batch * n, d)
    cs = ctx.reshape(batch * l, d)
    x_grp = dict(tm=tm_x, tpg=n, gbase=0)
    c_grp = dict(tm=tm_c, tpg=batch * l, gbase=batch)

    for layer in range(depth):
        kind = layer % N_MIXERS
        j = layer // N_MIXERS
        need_ctx = layer < depth - 1
        mod = mod_all[layer]
        g_mix = norm_mix_g[layer].reshape(1, d)
        ffn = (norm_ffn_g[layer].reshape(1, d), w1_all, w3_all, w2_all, layer)
        fin = final_g.reshape(1, d) if layer == depth - 1 else None

        if kind == 0:
            wqkv = (wqkv_all, j)
            wo = (wo_all, j)
            rp = _rpb_rows(na_rpb[j])
            qkv_x = _qkv_proj(xs, mod, g_mix, wqkv, q_scale=q_scale, **x_grp)
            qkv_c = _qkv_proj(cs, mod, g_mix, wqkv, q_scale=q_scale, **c_grp)
            o_x, o_c = _attention(qkv_x, qkv_c, rp, batch, need_ctx)
            xs = _proj_ffn(xs, o_x, mod, wo, ffn, fc=fc, final_g=fin, **x_grp)
            if need_ctx:
                cs = _proj_ffn(cs, o_c, mod, wo, ffn, fc=fc, **c_grp)
        elif kind == 1:
            w_in = (win_all, j)
            wo = (wout_all, j)
            gb_x, z_x = _conv_in_proj(xs, mod, g_mix, w_in, **x_grp)
            xs_new = _conv_ffn(xs, z_x, gb_x, sc_conv_w[j], mod, wo, ffn, fc=fc, seq=n, **x_grp)
            if need_ctx:
                gb_c, z_c = _conv_in_proj(cs, mod, g_mix, w_in, **c_grp)
                cs = _conv_ffn(cs, z_c, gb_c, sc_conv_w[j], mod, wo, ffn, fc=fc, seq=l, **c_grp)
            xs = xs_new
        else:
            pw = (pw_all, j)
            ps = pool_scale[j].reshape(1, d)
            xs_new = _pool_ffn(xs, mod, g_mix, pw, ps, ffn, fc=fc, seq=n, **x_grp)
            if need_ctx:
                cs = _pool_ffn(cs, mod, g_mix, pw, ps, ffn, fc=fc, seq=l, **c_grp)
            xs = xs_new
    return xs.reshape(batch, n, d)
```

```python
import functools

import numpy as np
import jax
import jax.numpy as jnp
from jax import lax
from jax.experimental import pallas as pl
from jax.experimental.pallas import tpu as pltpu

F32 = jnp.float32
BF16 = jnp.bfloat16

GRID_W = 64
N_MIXERS = 3
N_HEADS = 16
WIN_H = 8
WIN_W = 16
CONV_W = 3
POOL_WINDOWS = (2, 4, 8, 16)
N_MOD = 6
EPS = 1e-6
LOG2_E = 1.4426950408889634

V7X_VMEM_BYTES = 64 * 1024 * 1024
LANES = 128
F32_SUBLANES = 8
BF16_SUBLANES = 16

HEADS_PER_BLOCK = 2
MOD_ROWS = 8


def _vmem_limit(est_bytes):
    return int(min(V7X_VMEM_BYTES - 6 * 1024 * 1024, max(32 * 1024 * 1024, est_bytes * 5 // 4)))


def _params(est_bytes, ndims=1):
    return pltpu.CompilerParams(dimension_semantics=("arbitrary",) * ndims,
                                vmem_limit_bytes=_vmem_limit(est_bytes))


def _const_spec(shape):
    nd = len(shape)
    return pl.BlockSpec(shape, lambda *_: (0,) * nd, pipeline_mode=pl.Buffered(1))


def _layer_spec(stack, idx):
    nd = stack.ndim
    return pl.BlockSpec((None,) + stack.shape[1:], lambda *_: (idx,) + (0,) * (nd - 1),
                        pipeline_mode=pl.Buffered(1))


def _split_bf16(v):
    hi = v.astype(BF16)
    lo = (v - hi.astype(F32)).astype(BF16)
    return hi, lo


def _mod_kernel(c_ref, w_ref, b_ref, o_ref):
    c = c_ref[...]
    s = c * jax.nn.sigmoid(c)
    s_hi, s_lo = _split_bf16(s)
    lhs = jnp.concatenate([s_hi, s_lo], axis=0)
    w_hi, w_lo = _split_bf16(w_ref[...])
    r = (jnp.dot(lhs, w_hi, preferred_element_type=F32)
         + jnp.dot(lhs, w_lo, preferred_element_type=F32))
    o_ref[...] = r[:MOD_ROWS] + r[MOD_ROWS:] + b_ref[...]


def _modulation(c_rows, adaln_w, adaln_b):
    depth, d, nm = adaln_w.shape
    nc = nm // 4
    est = 2 * d * nc * 4 + 3 * d * nc * 2 + 4 * MOD_ROWS * nc * 4
    return pl.pallas_call(
        _mod_kernel,
        grid=(depth, nm // nc),
        in_specs=[
            pl.BlockSpec((MOD_ROWS, d), lambda l, j: (0, 0)),
            pl.BlockSpec((None, d, nc), lambda l, j: (l, 0, j)),
            pl.BlockSpec((None, 1, nc), lambda l, j: (l, 0, j)),
        ],
        out_specs=pl.BlockSpec((None, MOD_ROWS, nc), lambda l, j: (l, 0, j)),
        out_shape=jax.ShapeDtypeStruct((depth, MOD_ROWS, nm), F32),
        compiler_params=_params(est, 2),
        name="adaln_mod",
    )(c_rows, adaln_w, adaln_b.reshape(depth, 1, nm))


def _mod_row(mod_ref, tm, tokens_per_group, group_base):
    grp = group_base + lax.div(pl.program_id(0) * tm, tokens_per_group)
    return mod_ref[pl.ds(grp, 1), :]


def _chunk(m, idx, d):
    return m[:, idx * d:(idx + 1) * d]


def _rms(x, g):
    ms = jnp.mean(x * x, axis=-1, keepdims=True)
    return x * lax.rsqrt(ms + EPS) * g


def _norm_mod(x, g, shift, scale):
    return _rms(x, g) * (1.0 + scale) + shift


def _ffn_tail(x1, m, g_ffn, w1_ref, w3_ref, w2_ref, hid_ref, fc):
    d = x1.shape[-1]
    f = w1_ref.shape[-1]
    h = _norm_mod(x1, g_ffn, _chunk(m, 3, d), _chunk(m, 4, d)).astype(BF16)
    for c in range(f // fc):
        sl = slice(c * fc, (c + 1) * fc)
        a = jnp.dot(h, w1_ref[:, sl], preferred_element_type=F32)
        b = jnp.dot(h, w3_ref[:, sl], preferred_element_type=F32)
        hid_ref[:, sl] = (a * jax.nn.sigmoid(a) * b).astype(BF16)
    y = jnp.dot(hid_ref[...], w2_ref[...], preferred_element_type=F32)
    return x1 + _chunk(m, 5, d) * y


def _finish(out, final_g_ref, o_ref):
    if final_g_ref is not None:
        out = _rms(out, final_g_ref[...])
    o_ref[...] = out


def _cast_side_job(refs, n_cast, n_out):
    for src, dst in zip(refs[:n_cast], refs[n_cast + n_out:]):
        dst[...] = src[...].astype(BF16)
    return refs[n_cast:n_cast + n_out]


def _qkv_kernel(x_ref, mod_ref, g_ref, w_ref, *refs, tm, tpg, gbase, q_scale, n_cast):
    (o_ref,) = _cast_side_job(refs, n_cast, 1)
    d = x_ref.shape[-1]
    m = _mod_row(mod_ref, tm, tpg, gbase)
    h = _norm_mod(x_ref[...], g_ref[...], _chunk(m, 0, d), _chunk(m, 1, d)).astype(BF16)
    for j in range(3):
        r = jnp.dot(h, w_ref[:, j * d:(j + 1) * d], preferred_element_type=F32)
        if j == 0:
            r = r * q_scale
        o_ref[:, j * d:(j + 1) * d] = r.astype(BF16)


def _conv_in_kernel(x_ref, mod_ref, g_ref, w_ref, *refs, tm, tpg, gbase, n_cast):
    gb_ref, z_ref = _cast_side_job(refs, n_cast, 2)
    d = x_ref.shape[-1]
    m = _mod_row(mod_ref, tm, tpg, gbase)
    h = _norm_mod(x_ref[...], g_ref[...], _chunk(m, 0, d), _chunk(m, 1, d)).astype(BF16)
    gb_ref[...] = jnp.dot(h, w_ref[:, :d], preferred_element_type=F32).astype(BF16)
    gc = jnp.dot(h, w_ref[:, d:2 * d], preferred_element_type=F32)
    val = jnp.dot(h, w_ref[:, 2 * d:], preferred_element_type=F32)
    z_ref[...] = (gc * val).astype(BF16)


def _cast_rows_per_step(n_rows, n_steps):
    if n_rows % n_steps == 0 and (n_rows // n_steps) % BF16_SUBLANES == 0:
        return n_rows // n_steps
    return None


def _front_call(kernel, x, mod, g, w, out_shapes, out_specs, tm, name, casts):
    t, d = x.shape
    w, wi = w
    nout = w.shape[-1]
    steps = t // tm
    est = 2 * tm * d * 4 + d * nout * 2 + 2 * tm * nout * 2 + tm * nout * 4 + 4 * tm * d * 4
    cast_args, cast_in_specs, cast_out_specs, cast_shapes = [], [], [], []
    for stack, li in casts:
        _, r, c = stack.shape
        rb = _cast_rows_per_step(r, steps)
        cast_args.append(stack)
        cast_in_specs.append(pl.BlockSpec((None, rb, c), lambda i, li=li: (li, i, 0)))
        cast_out_specs.append(pl.BlockSpec((rb, c), lambda i: (i, 0)))
        cast_shapes.append(jax.ShapeDtypeStruct((r, c), BF16))
        est += 2 * rb * c * (4 + 2)
    outs = pl.pallas_call(
        functools.partial(kernel, n_cast=len(casts)),
        grid=(steps,),
        in_specs=[
            pl.BlockSpec((tm, d), lambda i: (i, 0)),
            _const_spec(mod.shape),
            _const_spec(g.shape),
            _layer_spec(w, wi),
        ] + cast_in_specs,
        out_specs=list(out_specs) + cast_out_specs,
        out_shape=list(out_shapes) + cast_shapes,
        compiler_params=_params(est),
        name=name,
    )(x, mod, g, w, *cast_args)
    n_own = len(out_shapes)
    return outs[:n_own], outs[n_own:]


def _qkv_proj(x, mod, g, w, tm, tpg, gbase, q_scale, casts=()):
    t, d = x.shape
    kern = functools.partial(_qkv_kernel, tm=tm, tpg=tpg, gbase=gbase, q_scale=q_scale)
    (qkv,), cast = _front_call(kern, x, mod, g, w, [jax.ShapeDtypeStruct((t, 3 * d), BF16)],
                               [pl.BlockSpec((tm, 3 * d), lambda i: (i, 0))], tm, "qkv_proj", casts)
    return qkv, cast


def _conv_in_proj(x, mod, g, w, tm, tpg, gbase, casts=()):
    t, d = x.shape
    kern = functools.partial(_conv_in_kernel, tm=tm, tpg=tpg, gbase=gbase)
    spec = pl.BlockSpec((tm, d), lambda i: (i, 0))
    (gb, z), cast = _front_call(kern, x, mod, g, w, [jax.ShapeDtypeStruct((t, d), BF16)] * 2, [spec, spec],
                                tm, "conv_in_proj", casts)
    return gb, z, cast


def _split_casts(layers, cast_outs, per_layer):
    return {li: tuple(cast_outs[per_layer * k:per_layer * (k + 1)]) for k, li in enumerate(layers)}


def _stack_heads(q):
    lane = lax.broadcasted_iota(jnp.int32, q.shape, 1)
    zero = jnp.zeros_like(q)
    return jnp.concatenate([jnp.where(lane < LANES // 2, q, zero),
                            jnp.where(lane >= LANES // 2, q, zero)], axis=0)


_NT = (((1,), (1,)), ((), ()))


QROWS = 4


def _band_plan(rows, kh):
    band = QROWS + kh
    assert rows % QROWS == 0 and band % 2 == 0 and rows >= band + QROWS

    def plan(r0):
        ub = int(np.clip(r0 - kh // 2, 0, rows - band))
        tab = np.full((QROWS, band), -1, np.int64)
        for ri in range(QROWS):
            r = r0 + ri
            start = int(np.clip(r - kh // 2, 0, rows - kh))
            for u in range(band):
                if start <= ub + u < start + kh:
                    tab[ri, u] = ub + u - r + WIN_H - 1
        return ub - r0, tab

    first, last = plan(0), plan(rows - QROWS)
    interior = plan(QROWS)
    for r0 in range(QROWS, rows - QROWS, QROWS):
        off, tab = plan(r0)
        assert off == interior[0] and (tab == interior[1]).all()
    return band, [first, interior, last]


def _build_bias(rp_ref, xm_ref, bias_ref, plans):
    half = LANES // 2
    shape = (GRID_W, LANES)
    qcol = lax.broadcasted_iota(jnp.int32, shape, 0)
    lane = lax.broadcasted_iota(jnp.int32, shape, 1)
    kcol = jnp.bitwise_and(lane, half - 1)
    col_start = jnp.clip(qcol - WIN_W // 2, 0, GRID_W - WIN_W)
    in_window = (kcol >= col_start) & (kcol < col_start + WIN_W)
    low = lane < half
    neg = jnp.full(shape, -jnp.inf, F32)
    n_dr = rp_ref.shape[1]
    for hh in range(HEADS_PER_BLOCK):
        for dr in range(n_dr):
            base = jnp.broadcast_to(rp_ref[hh, dr:dr + 1, :] * LOG2_E, shape)
            t_lo = pltpu.roll(base, 0, 1, stride=1, stride_axis=0)
            both = jnp.where(low, t_lo, pltpu.roll(t_lo, half, 1))
            xm_ref[hh, dr] = jnp.where(in_window, both, neg)
    for v, (_, tab) in enumerate(plans):
        for hh in range(HEADS_PER_BLOCK):
            for ri in range(QROWS):
                r_lo = (hh * QROWS + ri) * GRID_W
                for j in range(tab.shape[1] // 2):
                    da, db = int(tab[ri, 2 * j]), int(tab[ri, 2 * j + 1])
                    ta = xm_ref[hh, da] if da >= 0 else neg
                    tb = xm_ref[hh, db] if db >= 0 else neg
                    bias_ref[v, r_lo:r_lo + GRID_W, j * LANES:(j + 1) * LANES] = jnp.where(low, ta, tb)


SOFTMAX_SLAB = 16
BLOCKS_PER_STEP = 16


def _softmax_rows(s_parts, p_ref):
    n_rows = s_parts[0].shape[0]
    for r in range(0, n_rows, SOFTMAX_SLAB):
        rs = slice(r, r + SOFTMAX_SLAB)
        blocks = [s[rs, c:c + LANES] for s in s_parts for c in range(0, s.shape[1], LANES)]
        mx = jnp.max(functools.reduce(jnp.maximum, blocks), axis=-1, keepdims=True)
        for j, blk in enumerate(blocks):
            p_ref[rs, j * LANES:(j + 1) * LANES] = jnp.exp2(blk - mx).astype(BF16)


def _pv_heads(p_ref, n_q, pieces):
    half = LANES // 2
    outs = []
    for hh in range(HEADS_PER_BLOCK):
        hs = slice(hh * n_q, (hh + 1) * n_q)
        o = None
        for cols, v in pieces:
            own = (lax.broadcasted_iota(jnp.int32, v.shape, 1) < half) == (hh == 0)
            part = jnp.dot(p_ref[hs, cols], jnp.where(own, v, jnp.ones_like(v)), preferred_element_type=F32)
            o = part if o is None else o + part
        outs.append(o)
    low = lax.broadcasted_iota(jnp.int32, (n_q, LANES), 1) < half
    num = jnp.where(low, outs[0], outs[1])
    den = pltpu.roll(jnp.where(low, outs[1], outs[0]), half, 1)
    return num / den


def _attn_kernel(q_ref, k_ref, v_ref, qc_ref, kc_ref, vc_ref, rp_ref, *rest, rows, kh, need_ctx):
    o_ref = rest[0]
    oc_ref = rest[1] if need_ctx else None
    xm_ref, bias_ref = rest[-2 - BLOCKS_PER_STEP:-BLOCKS_PER_STEP]
    p_refs = rest[-BLOCKS_PER_STEP:]
    band, plans = _band_plan(rows, kh)
    nblk = rows // QROWS
    qn = QROWS * GRID_W
    kn = band * GRID_W
    lc = kc_ref.shape[0]

    @pl.when(pl.program_id(1) == 0)
    def _():
        _build_bias(rp_ref, xm_ref, bias_ref, plans)

    def scores(t):
        r0 = t * QROWS
        ub = jnp.clip(r0 - kh // 2, 0, rows - band)
        variant = jnp.where(t > 0, 1, 0) + jnp.where(t == nblk - 1, 1, 0)
        qoff = pl.multiple_of(r0 * GRID_W, qn)
        koff = pl.multiple_of(ub * GRID_W, GRID_W)
        qm = _stack_heads(q_ref[pl.ds(qoff, qn), :])
        s_lat = (lax.dot_general(qm, k_ref[pl.ds(koff, kn), :], _NT, preferred_element_type=F32)
                 + bias_ref[variant])
        s_ctx = lax.dot_general(qm, kc_ref[...], _NT, preferred_element_type=F32)
        return qoff, koff, s_lat, s_ctx

    def finish(blk, p_ref):
        qoff, koff, s_lat, s_ctx = blk
        _softmax_rows([s_lat, s_ctx], p_ref)
        pieces = [(slice(0, kn), v_ref[pl.ds(koff, kn), :]), (slice(kn, kn + lc), vc_ref[...])]
        o_ref[pl.ds(qoff, qn), :] = _pv_heads(p_ref, qn, pieces).astype(BF16)

    def step(tt, carry):
        blocks = [scores(tt * BLOCKS_PER_STEP + i) for i in range(BLOCKS_PER_STEP)]
        for blk, p_ref in zip(blocks, p_refs):
            finish(blk, p_ref)
        return carry

    lax.fori_loop(0, nblk // BLOCKS_PER_STEP, step, 0)

    if need_ctx:
        s = lax.dot_general(_stack_heads(qc_ref[...]), kc_ref[...], _NT, preferred_element_type=F32)
        _softmax_rows([s], p_refs[0])
        oc_ref[...] = _pv_heads(p_refs[0], lc, [(slice(0, lc), vc_ref[...])]).astype(BF16)


def _rpb_rows(rpb):
    h, n_dr, _ = rpb.shape
    gap = jnp.zeros((h, n_dr, LANES - (2 * WIN_W - 1)), F32)
    return jnp.concatenate([rpb[..., WIN_W - 1:].astype(F32), gap, rpb[..., :WIN_W - 1].astype(F32)], axis=-1)


def _attention(qkv_x, qkv_c, rp, batch, need_ctx):
    tx, d3 = qkv_x.shape
    d = d3 // 3
    n = tx // batch
    l = qkv_c.shape[0] // batch
    rows = n // GRID_W
    kh = min(WIN_H, rows)
    npair = d // LANES
    band, plans = _band_plan(rows, kh)
    n_dr = rp.shape[1]
    kern = functools.partial(_attn_kernel, rows=rows, kh=kh, need_ctx=need_ctx)
    in_specs = [
        pl.BlockSpec((n, LANES), lambda p, b: (b, p)),
        pl.BlockSpec((n, LANES), lambda p, b: (b, npair + p)),
        pl.BlockSpec((n, LANES), lambda p, b: (b, 2 * npair + p)),
        pl.BlockSpec((l, LANES), lambda p, b: (b, p)),
        pl.BlockSpec((l, LANES), lambda p, b: (b, npair + p)),
        pl.BlockSpec((l, LANES), lambda p, b: (b, 2 * npair + p)),
        pl.BlockSpec((HEADS_PER_BLOCK, n_dr, LANES), lambda p, b: (p, 0, 0)),
    ]
    out_shape = [jax.ShapeDtypeStruct((tx, d), BF16)]
    out_specs = [pl.BlockSpec((n, LANES), lambda p, b: (b, p))]
    if need_ctx:
        out_shape.append(jax.ShapeDtypeStruct((batch * l, d), BF16))
        out_specs.append(pl.BlockSpec((l, LANES), lambda p, b: (b, p)))
    qrows = HEADS_PER_BLOCK * QROWS * GRID_W
    assert (rows // QROWS) % BLOCKS_PER_STEP == 0 and HEADS_PER_BLOCK * l <= qrows
    bias_bytes = len(plans) * qrows * band * GRID_W * 4
    est = (2 * (4 * n * LANES * 2 + 4 * l * LANES * 2) + bias_bytes
           + HEADS_PER_BLOCK * n_dr * GRID_W * LANES * 4 + 6 * qrows * (band * GRID_W + l) * 4)
    outs = pl.pallas_call(
        kern,
        grid=(npair, batch),
        in_specs=in_specs,
        out_specs=out_specs,
        out_shape=out_shape,
        scratch_shapes=[pltpu.VMEM((HEADS_PER_BLOCK, n_dr, GRID_W, LANES), F32),
                        pltpu.VMEM((len(plans), qrows, band * GRID_W), F32)]
        + [pltpu.VMEM((qrows, band * GRID_W + l), BF16)] * BLOCKS_PER_STEP,
        compiler_params=_params(est, 2),
        name="nbr_attention",
    )(qkv_x, qkv_x, qkv_x, qkv_c, qkv_c, qkv_c, rp)
    return (outs[0], outs[1]) if need_ctx else (outs[0], None)


def _proj_ffn_kernel(x_ref, a_ref, mod_ref, wo_ref, g_ref, w1_ref, w3_ref, w2_ref, *rest,
                     tm, tpg, gbase, fc, final):
    final_g_ref = rest[0] if final else None
    o_ref, hid_ref = rest[-2:]
    d = x_ref.shape[-1]
    m = _mod_row(mod_ref, tm, tpg, gbase)
    y = jnp.dot(a_ref[...], wo_ref[...], preferred_element_type=F32)
    x1 = x_ref[...] + _chunk(m, 2, d) * y
    _finish(_ffn_tail(x1, m, g_ref[...], w1_ref, w3_ref, w2_ref, hid_ref, fc), final_g_ref, o_ref)


def _conv_ffn_kernel(x_ref, z_ref, zp_ref, zn_ref, gb_ref, cw_ref, mod_ref, wo_ref, g_ref,
                     w1_ref, w3_ref, w2_ref, o_ref, hid_ref, *, tm, tpg, gbase, fc, seq):
    d = x_ref.shape[-1]
    i = pl.program_id(0)
    m = _mod_row(mod_ref, tm, tpg, gbase)
    pos0 = lax.rem(i * tm, seq)
    z = z_ref[...].astype(F32)
    prev = zp_ref[...].astype(F32)[BF16_SUBLANES - 1:, :]
    nxt = zn_ref[...].astype(F32)[:1, :]
    prev = jnp.where(pos0 == 0, jnp.zeros_like(prev), prev)
    nxt = jnp.where(pos0 + tm == seq, jnp.zeros_like(nxt), nxt)
    row = lax.broadcasted_iota(jnp.int32, z.shape, 0)
    z_m1 = jnp.where(row == 0, prev, pltpu.roll(z, 1, 0))
    z_p1 = jnp.where(row == tm - 1, nxt, pltpu.roll(z, tm - 1, 0))
    cw = cw_ref[...]
    conv = z_m1 * cw[0:1] + z * cw[1:2] + z_p1 * cw[2:3]
    a = (gb_ref[...].astype(F32) * conv).astype(BF16)
    y = jnp.dot(a, wo_ref[...], preferred_element_type=F32)
    x1 = x_ref[...] + _chunk(m, 2, d) * y
    _finish(_ffn_tail(x1, m, g_ref[...], w1_ref, w3_ref, w2_ref, hid_ref, fc), None, o_ref)


POOL_ROW_CHUNKS = 2


def _pool_ffn_kernel(x_ref, xp_ref, xn_ref, mod_ref, gm_ref, pw_ref, ps_ref, g_ref,
                     w1_ref, w3_ref, w2_ref, o_ref, hid_ref, hext_ref, *, tm, tpg, gbase, fc, seq):
    d = x_ref.shape[-1]
    halo = F32_SUBLANES
    i = pl.program_id(0)
    m = _mod_row(mod_ref, tm, tpg, gbase)
    shift, scale = _chunk(m, 0, d), _chunk(m, 1, d)
    gm = gm_ref[...]
    pos0 = lax.rem(i * tm, seq)
    x = x_ref[...]
    h_top = _norm_mod(xp_ref[...], gm, shift, scale)
    h_bot = _norm_mod(xn_ref[...], gm, shift, scale)
    hext_ref[0:halo, :] = jnp.where(pos0 == 0, jnp.zeros_like(h_top), h_top)
    hext_ref[halo:halo + tm, :] = _norm_mod(x, gm, shift, scale)
    hext_ref[halo + tm:, :] = jnp.where(pos0 + tm == seq, jnp.zeros_like(h_bot), h_bot)

    gc = d // len(POOL_WINDOWS)
    rc = tm // POOL_ROW_CHUNKS
    for r0 in range(0, tm, rc):
        pos = pos0 + r0 + lax.broadcasted_iota(jnp.int32, (rc, 1), 0)
        pieces = []
        for gi, win in enumerate(POOL_WINDOWS):
            cols = slice(gi * gc, (gi + 1) * gc)
            acc = hext_ref[pl.ds(halo + r0 - win // 2, rc), cols]
            for dd in range(-win // 2 + 1, win // 2):
                acc = acc + hext_ref[pl.ds(halo + r0 + dd, rc), cols]
            cnt = (jnp.minimum(pos + win // 2, seq) - jnp.maximum(pos - win // 2, 0)).astype(F32)
            pooled = (acc / cnt - hext_ref[pl.ds(halo + r0, rc), cols]).astype(BF16)
            pieces.append(jnp.dot(pooled, pw_ref[gi], preferred_element_type=F32))
        y = jnp.concatenate(pieces, axis=-1) * ps_ref[...]
        x1 = x_ref[pl.ds(r0, rc), :] + _chunk(m, 2, d) * y
        out = _ffn_tail(x1, m, g_ref[...], w1_ref, w3_ref, w2_ref, hid_ref.at[pl.ds(r0, rc), :], fc)
        o_ref[pl.ds(r0, rc), :] = out


def _ffn_est(tm, d, f):
    return 3 * d * f * 2 + 4 * tm * d * 4 + tm * f * 2 + 6 * tm * d * 4 + 3 * tm * 512 * 4


def _ffn_consts(g_ffn, w1, w3, w2):
    return ([g_ffn, w1, w3, w2],
            [_const_spec(g_ffn.shape), _const_spec(w1.shape), _const_spec(w3.shape), _const_spec(w2.shape)])


def _proj_ffn(x, a, mod, wo, ffn, tm, tpg, gbase, fc, final_g=None):
    t, d = x.shape
    f = ffn[1].shape[-1]
    wo, woi = wo
    tile = pl.BlockSpec((tm, d), lambda i: (i, 0))
    fargs, fspecs = _ffn_consts(*ffn)
    args = [x, a, mod, wo] + fargs
    specs = [tile, tile, _const_spec(mod.shape), _layer_spec(wo, woi)] + fspecs
    if final_g is not None:
        args.append(final_g)
        specs.append(_const_spec(final_g.shape))
    kern = functools.partial(_proj_ffn_kernel, tm=tm, tpg=tpg, gbase=gbase, fc=fc, final=final_g is not None)
    return pl.pallas_call(
        kern, grid=(t // tm,), in_specs=specs, out_specs=tile,
        out_shape=jax.ShapeDtypeStruct((t, d), F32),
        scratch_shapes=[pltpu.VMEM((tm, f), BF16)],
        compiler_params=_params(_ffn_est(tm, d, f) + d * d * 2 + 2 * tm * d * 2),
        name="proj_ffn",
    )(*args)


def _conv_ffn(x, z, gb, conv_w, mod, wo, ffn, tm, tpg, gbase, fc, seq):
    t, d = x.shape
    f = ffn[1].shape[-1]
    wo, woi = wo
    tile = pl.BlockSpec((tm, d), lambda i: (i, 0))
    hb = tm // BF16_SUBLANES
    nhb = t // BF16_SUBLANES
    prev = pl.BlockSpec((BF16_SUBLANES, d), lambda i: (jnp.maximum(i * hb - 1, 0), 0))
    nxt = pl.BlockSpec((BF16_SUBLANES, d), lambda i: (jnp.minimum((i + 1) * hb, nhb - 1), 0))
    fargs, fspecs = _ffn_consts(*ffn)
    kern = functools.partial(_conv_ffn_kernel, tm=tm, tpg=tpg, gbase=gbase, fc=fc, seq=seq)
    return pl.pallas_call(
        kern, grid=(t // tm,),
        in_specs=[tile, tile, prev, nxt, tile, _const_spec(conv_w.shape), _const_spec(mod.shape),
                  _layer_spec(wo, woi)] + fspecs,
        out_specs=tile,
        out_shape=jax.ShapeDtypeStruct((t, d), F32),
        scratch_shapes=[pltpu.VMEM((tm, f), BF16)],
        compiler_params=_params(_ffn_est(tm, d, f) + d * d * 2 + 4 * tm * d * 2 + 4 * tm * d * 4),
        name="conv_ffn",
    )(x, z, z, z, gb, conv_w, mod, wo, *fargs)


def _pool_ffn(x, mod, g_mix, pool_w, pool_scale, ffn, tm, tpg, gbase, fc, seq):
    t, d = x.shape
    f = ffn[1].shape[-1]
    pool_w, pwi = pool_w
    tile = pl.BlockSpec((tm, d), lambda i: (i, 0))
    hb = tm // F32_SUBLANES
    nhb = t // F32_SUBLANES
    prev = pl.BlockSpec((F32_SUBLANES, d), lambda i: (jnp.maximum(i * hb - 1, 0), 0))
    nxt = pl.BlockSpec((F32_SUBLANES, d), lambda i: (jnp.minimum((i + 1) * hb, nhb - 1), 0))
    fargs, fspecs = _ffn_consts(*ffn)
    kern = functools.partial(_pool_ffn_kernel, tm=tm, tpg=tpg, gbase=gbase, fc=fc, seq=seq)
    return pl.pallas_call(
        kern, grid=(t // tm,),
        in_specs=[tile, prev, nxt, _const_spec(mod.shape), _const_spec(g_mix.shape),
                  _layer_spec(pool_w, pwi), _const_spec(pool_scale.shape)] + fspecs,
        out_specs=tile,
        out_shape=jax.ShapeDtypeStruct((t, d), F32),
        scratch_shapes=[pltpu.VMEM((tm, f), BF16), pltpu.VMEM((tm + 2 * F32_SUBLANES, d), F32)],
        compiler_params=_params(_ffn_est(tm, d, f) + 3 * tm * d * 4),
        name="pool_ffn",
    )(x, x, x, mod, g_mix, pool_w, pool_scale, *fargs)


def _token_tile(seq):
    return min(seq, 1024)


def kernel(x, c, ctx, c_ctx, adaln_w, adaln_b, norm_mix_g, norm_ffn_g, ffn_w1, ffn_w3, ffn_w2,
           na_w_qkv, na_w_o, na_rpb, sc_w_in, sc_conv_w, sc_w_out, pool_w, pool_scale, final_g):
    batch, n, d = x.shape
    l = ctx.shape[1]
    depth = adaln_w.shape[0]
    f = ffn_w1.shape[-1]
    assert batch + 1 <= MOD_ROWS and d % LANES == 0 and n % GRID_W == 0
    assert d // N_HEADS * HEADS_PER_BLOCK == LANES
    fc = 256
    assert f % fc == 0
    assert (depth - 1) % N_MIXERS == 0, "the final norm is fused into the attention-layer tail"
    tm_x, tm_c = _token_tile(n), _token_tile(l)
    q_scale = float(d // N_HEADS) ** -0.5 * LOG2_E

    c_rows = jnp.concatenate([c, c_ctx[None, :], jnp.zeros((MOD_ROWS - batch - 1, d), F32)], axis=0)
    mod_all = _modulation(c_rows, adaln_w, adaln_b)

    wqkv_all, wo_all = na_w_qkv.astype(BF16), na_w_o.astype(BF16)
    win_all, wout_all = sc_w_in.astype(BF16), sc_w_out.astype(BF16)
    pw_all = pool_w.astype(BF16)
    ffn_stacks = (ffn_w1, ffn_w3, ffn_w2)
    steps_x = batch * n // tm_x
    side_ok = all(_cast_rows_per_step(st.shape[1], steps_x) is not None for st in ffn_stacks)
    cast_plan, ffn_bf16 = {}, {}
    host = None
    for layer in range(depth):
        if layer % N_MIXERS != N_MIXERS - 1 and side_ok:
            host = layer
        if host is None:
            ffn_bf16[layer] = tuple(st[layer].astype(BF16) for st in ffn_stacks)
        else:
            cast_plan.setdefault(host, []).append(layer)

    xs = x.reshape(batch * n, d)
    cs = ctx.reshape(batch * l, d)
    x_grp = dict(tm=tm_x, tpg=n, gbase=0)
    c_grp = dict(tm=tm_c, tpg=batch * l, gbase=batch)

    for layer in range(depth):
        kind = layer % N_MIXERS
        j = layer // N_MIXERS
        need_ctx = layer < depth - 1
        mod = mod_all[layer]
        g_mix = norm_mix_g[layer].reshape(1, d)
        g_ffn = norm_ffn_g[layer].reshape(1, d)
        fin = final_g.reshape(1, d) if layer == depth - 1 else None
        hosted = cast_plan.get(layer, [])
        casts = [(st, li) for li in hosted for st in ffn_stacks]

        if kind == 0:
            wqkv = (wqkv_all, j)
            wo = (wo_all, j)
            rp = _rpb_rows(na_rpb[j])
            qkv_x, cast_outs = _qkv_proj(xs, mod, g_mix, wqkv, q_scale=q_scale, casts=casts, **x_grp)
            ffn_bf16.update(_split_casts(hosted, cast_outs, len(ffn_stacks)))
            qkv_c, _ = _qkv_proj(cs, mod, g_mix, wqkv, q_scale=q_scale, **c_grp)
            ffn = (g_ffn,) + ffn_bf16[layer]
            o_x, o_c = _attention(qkv_x, qkv_c, rp, batch, need_ctx)
            xs = _proj_ffn(xs, o_x, mod, wo, ffn, fc=fc, final_g=fin, **x_grp)
            if need_ctx:
                cs = _proj_ffn(cs, o_c, mod, wo, ffn, fc=fc, **c_grp)
        elif kind == 1:
            w_in = (win_all, j)
            wo = (wout_all, j)
            gb_x, z_x, cast_outs = _conv_in_proj(xs, mod, g_mix, w_in, casts=casts, **x_grp)
            ffn_bf16.update(_split_casts(hosted, cast_outs, len(ffn_stacks)))
            ffn = (g_ffn,) + ffn_bf16[layer]
            xs_new = _conv_ffn(xs, z_x, gb_x, sc_conv_w[j], mod, wo, ffn, fc=fc, seq=n, **x_grp)
            if need_ctx:
                gb_c, z_c, _ = _conv_in_proj(cs, mod, g_mix, w_in, **c_grp)
                cs = _conv_ffn(cs, z_c, gb_c, sc_conv_w[j], mod, wo, ffn, fc=fc, seq=l, **c_grp)
            xs = xs_new
        else:
            pw = (pw_all, j)
            ps = pool_scale[j].reshape(1, d)
            ffn = (g_ffn,) + ffn_bf16[layer]
            xs_new = _pool_ffn(xs, mod, g_mix, pw, ps, ffn, fc=fc, seq=n, **x_grp)
            if need_ctx:
                cs = _pool_ffn(cs, mod, g_mix, pw, ps, ffn, fc=fc, seq=l, **c_grp)
            xs = xs_new
    return xs.reshape(batch, n, d)
```

```python
import functools

import numpy as np
import jax
import jax.numpy as jnp
from jax import lax
from jax.experimental import pallas as pl
from jax.experimental.pallas import tpu as pltpu

F32 = jnp.float32
BF16 = jnp.bfloat16

GRID_W = 64
N_MIXERS = 3
N_HEADS = 16
WIN_H = 8
WIN_W = 16
CONV_W = 3
POOL_WINDOWS = (2, 4, 8, 16)
N_MOD = 6
EPS = 1e-6
LOG2_E = 1.4426950408889634

V7X_VMEM_BYTES = 64 * 1024 * 1024
LANES = 128
F32_SUBLANES = 8
BF16_SUBLANES = 16

HEADS_PER_BLOCK = 2
MOD_ROWS = 8


def _vmem_limit(est_bytes):
    return int(min(V7X_VMEM_BYTES - 6 * 1024 * 1024, max(32 * 1024 * 1024, est_bytes * 5 // 4)))


def _params(est_bytes, ndims=1):
    return pltpu.CompilerParams(dimension_semantics=("arbitrary",) * ndims,
                                vmem_limit_bytes=_vmem_limit(est_bytes))


def _const_spec(shape):
    nd = len(shape)
    return pl.BlockSpec(shape, lambda *_: (0,) * nd, pipeline_mode=pl.Buffered(1))


def _layer_spec(stack, idx):
    nd = stack.ndim
    return pl.BlockSpec((None,) + stack.shape[1:], lambda *_: (idx,) + (0,) * (nd - 1),
                        pipeline_mode=pl.Buffered(1))


def _split_bf16(v):
    hi = v.astype(BF16)
    lo = (v - hi.astype(F32)).astype(BF16)
    return hi, lo


def _mod_kernel(c_ref, w_ref, b_ref, o_ref):
    c = c_ref[...]
    s = c * jax.nn.sigmoid(c)
    s_hi, s_lo = _split_bf16(s)
    lhs = jnp.concatenate([s_hi, s_lo], axis=0)
    w_hi, w_lo = _split_bf16(w_ref[...])
    r = (jnp.dot(lhs, w_hi, preferred_element_type=F32)
         + jnp.dot(lhs, w_lo, preferred_element_type=F32))
    o_ref[...] = r[:MOD_ROWS] + r[MOD_ROWS:] + b_ref[...]


def _modulation(c_rows, adaln_w, adaln_b):
    depth, d, nm = adaln_w.shape
    nc = nm // 4
    est = 2 * d * nc * 4 + 3 * d * nc * 2 + 4 * MOD_ROWS * nc * 4
    return pl.pallas_call(
        _mod_kernel,
        grid=(depth, nm // nc),
        in_specs=[
            pl.BlockSpec((MOD_ROWS, d), lambda l, j: (0, 0)),
            pl.BlockSpec((None, d, nc), lambda l, j: (l, 0, j)),
            pl.BlockSpec((None, 1, nc), lambda l, j: (l, 0, j)),
        ],
        out_specs=pl.BlockSpec((None, MOD_ROWS, nc), lambda l, j: (l, 0, j)),
        out_shape=jax.ShapeDtypeStruct((depth, MOD_ROWS, nm), F32),
        compiler_params=_params(est, 2),
        name="adaln_mod",
    )(c_rows, adaln_w, adaln_b.reshape(depth, 1, nm))


def _mod_row(mod_ref, tm, tokens_per_group, group_base):
    grp = group_base + lax.div(pl.program_id(0) * tm, tokens_per_group)
    return mod_ref[pl.ds(grp, 1), :]


def _chunk(m, idx, d):
    return m[:, idx * d:(idx + 1) * d]


def _rms(x, g):
    ms = jnp.mean(x * x, axis=-1, keepdims=True)
    return x * lax.rsqrt(ms + EPS) * g


def _norm_mod(x, g, shift, scale):
    return _rms(x, g) * (1.0 + scale) + shift


def _ffn_tail(x1, m, g_ffn, w1_ref, w3_ref, w2_ref, hid_ref, fc, side_jobs=()):
    d = x1.shape[-1]
    f = w1_ref.shape[-1]
    side_jobs = list(side_jobs)
    h = _norm_mod(x1, g_ffn, _chunk(m, 3, d), _chunk(m, 4, d)).astype(BF16)
    for c in range(f // fc):
        sl = slice(c * fc, (c + 1) * fc)
        a = jnp.dot(h, w1_ref[:, sl], preferred_element_type=F32)
        b = jnp.dot(h, w3_ref[:, sl], preferred_element_type=F32)
        hid_ref[:, sl] = (a * jax.nn.sigmoid(a) * b).astype(BF16)
        if side_jobs:
            side_jobs.pop(0)()
    while side_jobs:
        side_jobs.pop(0)()
    y = jnp.dot(hid_ref[...], w2_ref[...], preferred_element_type=F32)
    return x1 + _chunk(m, 5, d) * y


def _finish(out, final_g_ref, o_ref):
    if final_g_ref is not None:
        out = _rms(out, final_g_ref[...])
    o_ref[...] = out


def _cast_side_job(refs, n_cast, n_out):
    for src, dst in zip(refs[:n_cast], refs[n_cast + n_out:]):
        dst[...] = src[...].astype(BF16)
    return refs[n_cast:n_cast + n_out]


def _qkv_kernel(x_ref, mod_ref, g_ref, w_ref, *refs, tm, tpg, gbase, q_scale, n_cast):
    (o_ref,) = _cast_side_job(refs, n_cast, 1)
    d = x_ref.shape[-1]
    m = _mod_row(mod_ref, tm, tpg, gbase)
    h = _norm_mod(x_ref[...], g_ref[...], _chunk(m, 0, d), _chunk(m, 1, d)).astype(BF16)
    for j in range(3):
        r = jnp.dot(h, w_ref[:, j * d:(j + 1) * d], preferred_element_type=F32)
        if j == 0:
            r = r * q_scale
        o_ref[:, j * d:(j + 1) * d] = r.astype(BF16)


def _conv_in_kernel(x_ref, mod_ref, g_ref, w_ref, *refs, tm, tpg, gbase, n_cast):
    gb_ref, z_ref = _cast_side_job(refs, n_cast, 2)
    d = x_ref.shape[-1]
    m = _mod_row(mod_ref, tm, tpg, gbase)
    h = _norm_mod(x_ref[...], g_ref[...], _chunk(m, 0, d), _chunk(m, 1, d)).astype(BF16)
    gb_ref[...] = jnp.dot(h, w_ref[:, :d], preferred_element_type=F32).astype(BF16)
    gc = jnp.dot(h, w_ref[:, d:2 * d], preferred_element_type=F32)
    val = jnp.dot(h, w_ref[:, 2 * d:], preferred_element_type=F32)
    z_ref[...] = (gc * val).astype(BF16)


def _cast_rows_per_step(n_rows, n_steps):
    if n_rows % n_steps == 0 and (n_rows // n_steps) % BF16_SUBLANES == 0:
        return n_rows // n_steps
    return None


def _front_call(kernel, x, mod, g, w, out_shapes, out_specs, tm, name, casts):
    t, d = x.shape
    w, wi = w
    nout = w.shape[-1]
    steps = t // tm
    est = 2 * tm * d * 4 + d * nout * 2 + 2 * tm * nout * 2 + tm * nout * 4 + 4 * tm * d * 4
    cast_args, cast_in_specs, cast_out_specs, cast_shapes = [], [], [], []
    for stack, li in casts:
        _, r, c = stack.shape
        rb = _cast_rows_per_step(r, steps)
        cast_args.append(stack)
        cast_in_specs.append(pl.BlockSpec((None, rb, c), lambda i, li=li: (li, i, 0)))
        cast_out_specs.append(pl.BlockSpec((rb, c), lambda i: (i, 0)))
        cast_shapes.append(jax.ShapeDtypeStruct((r, c), BF16))
        est += 2 * rb * c * (4 + 2)
    outs = pl.pallas_call(
        functools.partial(kernel, n_cast=len(casts)),
        grid=(steps,),
        in_specs=[
            pl.BlockSpec((tm, d), lambda i: (i, 0)),
            _const_spec(mod.shape),
            _const_spec(g.shape),
            _layer_spec(w, wi),
        ] + cast_in_specs,
        out_specs=list(out_specs) + cast_out_specs,
        out_shape=list(out_shapes) + cast_shapes,
        compiler_params=_params(est),
        name=name,
    )(x, mod, g, w, *cast_args)
    n_own = len(out_shapes)
    return outs[:n_own], outs[n_own:]


def _qkv_proj(x, mod, g, w, tm, tpg, gbase, q_scale, casts=()):
    t, d = x.shape
    kern = functools.partial(_qkv_kernel, tm=tm, tpg=tpg, gbase=gbase, q_scale=q_scale)
    (qkv,), cast = _front_call(kern, x, mod, g, w, [jax.ShapeDtypeStruct((t, 3 * d), BF16)],
                               [pl.BlockSpec((tm, 3 * d), lambda i: (i, 0))], tm, "qkv_proj", casts)
    return qkv, cast


def _conv_in_proj(x, mod, g, w, tm, tpg, gbase, casts=()):
    t, d = x.shape
    kern = functools.partial(_conv_in_kernel, tm=tm, tpg=tpg, gbase=gbase)
    spec = pl.BlockSpec((tm, d), lambda i: (i, 0))
    (gb, z), cast = _front_call(kern, x, mod, g, w, [jax.ShapeDtypeStruct((t, d), BF16)] * 2, [spec, spec],
                                tm, "conv_in_proj", casts)
    return gb, z, cast


def _split_casts(layers, cast_outs, per_layer):
    return {li: tuple(cast_outs[per_layer * k:per_layer * (k + 1)]) for k, li in enumerate(layers)}


def _stack_heads(q):
    lane = lax.broadcasted_iota(jnp.int32, q.shape, 1)
    zero = jnp.zeros_like(q)
    return jnp.concatenate([jnp.where(lane < LANES // 2, q, zero),
                            jnp.where(lane >= LANES // 2, q, zero)], axis=0)


_NT = (((1,), (1,)), ((), ()))


QROWS = 4


def _band_plan(rows, kh):
    band = QROWS + kh
    assert rows % QROWS == 0 and band % 2 == 0 and rows >= band + QROWS

    def plan(r0):
        ub = int(np.clip(r0 - kh // 2, 0, rows - band))
        tab = np.full((QROWS, band), -1, np.int64)
        for ri in range(QROWS):
            r = r0 + ri
            start = int(np.clip(r - kh // 2, 0, rows - kh))
            for u in range(band):
                if start <= ub + u < start + kh:
                    tab[ri, u] = ub + u - r + WIN_H - 1
        return ub - r0, tab

    first, last = plan(0), plan(rows - QROWS)
    interior = plan(QROWS)
    for r0 in range(QROWS, rows - QROWS, QROWS):
        off, tab = plan(r0)
        assert off == interior[0] and (tab == interior[1]).all()
    return band, [first, interior, last]


def _build_bias(rp_ref, xm_ref, bias_ref, plans):
    half = LANES // 2
    shape = (GRID_W, LANES)
    qcol = lax.broadcasted_iota(jnp.int32, shape, 0)
    lane = lax.broadcasted_iota(jnp.int32, shape, 1)
    kcol = jnp.bitwise_and(lane, half - 1)
    col_start = jnp.clip(qcol - WIN_W // 2, 0, GRID_W - WIN_W)
    in_window = (kcol >= col_start) & (kcol < col_start + WIN_W)
    low = lane < half
    neg = jnp.full(shape, -jnp.inf, F32)
    n_dr = rp_ref.shape[1]
    for hh in range(HEADS_PER_BLOCK):
        for dr in range(n_dr):
            base = jnp.broadcast_to(rp_ref[hh, dr:dr + 1, :] * LOG2_E, shape)
            t_lo = pltpu.roll(base, 0, 1, stride=1, stride_axis=0)
            both = jnp.where(low, t_lo, pltpu.roll(t_lo, half, 1))
            xm_ref[hh, dr] = jnp.where(in_window, both, neg)
    for v, (_, tab) in enumerate(plans):
        for hh in range(HEADS_PER_BLOCK):
            for ri in range(QROWS):
                r_lo = (hh * QROWS + ri) * GRID_W
                for j in range(tab.shape[1] // 2):
                    da, db = int(tab[ri, 2 * j]), int(tab[ri, 2 * j + 1])
                    ta = xm_ref[hh, da] if da >= 0 else neg
                    tb = xm_ref[hh, db] if db >= 0 else neg
                    bias_ref[v, r_lo:r_lo + GRID_W, j * LANES:(j + 1) * LANES] = jnp.where(low, ta, tb)


SOFTMAX_SLAB = 16
BLOCKS_PER_STEP = 16


def _softmax_rows(s_parts, p_ref):
    n_rows = s_parts[0].shape[0]
    for r in range(0, n_rows, SOFTMAX_SLAB):
        rs = slice(r, r + SOFTMAX_SLAB)
        blocks = [s[rs, c:c + LANES] for s in s_parts for c in range(0, s.shape[1], LANES)]
        mx = jnp.max(functools.reduce(jnp.maximum, blocks), axis=-1, keepdims=True)
        for j, blk in enumerate(blocks):
            p_ref[rs, j * LANES:(j + 1) * LANES] = jnp.exp2(blk - mx).astype(BF16)


def _pv_heads(p_ref, n_q, pieces):
    half = LANES // 2
    outs = []
    for hh in range(HEADS_PER_BLOCK):
        hs = slice(hh * n_q, (hh + 1) * n_q)
        o = None
        for cols, v in pieces:
            own = (lax.broadcasted_iota(jnp.int32, v.shape, 1) < half) == (hh == 0)
            part = jnp.dot(p_ref[hs, cols], jnp.where(own, v, jnp.ones_like(v)), preferred_element_type=F32)
            o = part if o is None else o + part
        outs.append(o)
    low = lax.broadcasted_iota(jnp.int32, (n_q, LANES), 1) < half
    num = jnp.where(low, outs[0], outs[1])
    den = pltpu.roll(jnp.where(low, outs[1], outs[0]), half, 1)
    return num / den


def _attn_kernel(q_ref, k_ref, v_ref, qc_ref, kc_ref, vc_ref, rp_ref, *rest, rows, kh, need_ctx):
    o_ref = rest[0]
    oc_ref = rest[1] if need_ctx else None
    xm_ref, bias_ref = rest[-2 - BLOCKS_PER_STEP:-BLOCKS_PER_STEP]
    p_refs = rest[-BLOCKS_PER_STEP:]
    band, plans = _band_plan(rows, kh)
    nblk = rows // QROWS
    qn = QROWS * GRID_W
    kn = band * GRID_W
    lc = kc_ref.shape[0]

    @pl.when(pl.program_id(1) == 0)
    def _():
        _build_bias(rp_ref, xm_ref, bias_ref, plans)

    def scores(t):
        r0 = t * QROWS
        ub = jnp.clip(r0 - kh // 2, 0, rows - band)
        variant = jnp.where(t > 0, 1, 0) + jnp.where(t == nblk - 1, 1, 0)
        qoff = pl.multiple_of(r0 * GRID_W, qn)
        koff = pl.multiple_of(ub * GRID_W, GRID_W)
        qm = _stack_heads(q_ref[pl.ds(qoff, qn), :])
        s_lat = (lax.dot_general(qm, k_ref[pl.ds(koff, kn), :], _NT, preferred_element_type=F32)
                 + bias_ref[variant])
        s_ctx = lax.dot_general(qm, kc_ref[...], _NT, preferred_element_type=F32)
        return qoff, koff, s_lat, s_ctx

    def finish(blk, p_ref):
        qoff, koff, s_lat, s_ctx = blk
        _softmax_rows([s_lat, s_ctx], p_ref)
        pieces = [(slice(0, kn), v_ref[pl.ds(koff, kn), :]), (slice(kn, kn + lc), vc_ref[...])]
        o_ref[pl.ds(qoff, qn), :] = _pv_heads(p_ref, qn, pieces).astype(BF16)

    def step(tt, carry):
        blocks = [scores(tt * BLOCKS_PER_STEP + i) for i in range(BLOCKS_PER_STEP)]
        for blk, p_ref in zip(blocks, p_refs):
            finish(blk, p_ref)
        return carry

    lax.fori_loop(0, nblk // BLOCKS_PER_STEP, step, 0)

    if need_ctx:
        s = lax.dot_general(_stack_heads(qc_ref[...]), kc_ref[...], _NT, preferred_element_type=F32)
        _softmax_rows([s], p_refs[0])
        oc_ref[...] = _pv_heads(p_refs[0], lc, [(slice(0, lc), vc_ref[...])]).astype(BF16)


def _rpb_rows(rpb):
    h, n_dr, _ = rpb.shape
    gap = jnp.zeros((h, n_dr, LANES - (2 * WIN_W - 1)), F32)
    return jnp.concatenate([rpb[..., WIN_W - 1:].astype(F32), gap, rpb[..., :WIN_W - 1].astype(F32)], axis=-1)


def _attention(qkv_x, qkv_c, rp, batch, need_ctx):
    tx, d3 = qkv_x.shape
    d = d3 // 3
    n = tx // batch
    l = qkv_c.shape[0] // batch
    rows = n // GRID_W
    kh = min(WIN_H, rows)
    npair = d // LANES
    band, plans = _band_plan(rows, kh)
    n_dr = rp.shape[1]
    kern = functools.partial(_attn_kernel, rows=rows, kh=kh, need_ctx=need_ctx)
    in_specs = [
        pl.BlockSpec((n, LANES), lambda p, b: (b, p)),
        pl.BlockSpec((n, LANES), lambda p, b: (b, npair + p)),
        pl.BlockSpec((n, LANES), lambda p, b: (b, 2 * npair + p)),
        pl.BlockSpec((l, LANES), lambda p, b: (b, p)),
        pl.BlockSpec((l, LANES), lambda p, b: (b, npair + p)),
        pl.BlockSpec((l, LANES), lambda p, b: (b, 2 * npair + p)),
        pl.BlockSpec((HEADS_PER_BLOCK, n_dr, LANES), lambda p, b: (p, 0, 0)),
    ]
    out_shape = [jax.ShapeDtypeStruct((tx, d), BF16)]
    out_specs = [pl.BlockSpec((n, LANES), lambda p, b: (b, p))]
    if need_ctx:
        out_shape.append(jax.ShapeDtypeStruct((batch * l, d), BF16))
        out_specs.append(pl.BlockSpec((l, LANES), lambda p, b: (b, p)))
    qrows = HEADS_PER_BLOCK * QROWS * GRID_W
    assert (rows // QROWS) % BLOCKS_PER_STEP == 0 and HEADS_PER_BLOCK * l <= qrows
    bias_bytes = len(plans) * qrows * band * GRID_W * 4
    est = (2 * (4 * n * LANES * 2 + 4 * l * LANES * 2) + bias_bytes
           + HEADS_PER_BLOCK * n_dr * GRID_W * LANES * 4
           + BLOCKS_PER_STEP * qrows * (band * GRID_W + l) * (2 + 4))
    outs = pl.pallas_call(
        kern,
        grid=(npair, batch),
        in_specs=in_specs,
        out_specs=out_specs,
        out_shape=out_shape,
        scratch_shapes=[pltpu.VMEM((HEADS_PER_BLOCK, n_dr, GRID_W, LANES), F32),
                        pltpu.VMEM((len(plans), qrows, band * GRID_W), F32)]
        + [pltpu.VMEM((qrows, band * GRID_W + l), BF16)] * BLOCKS_PER_STEP,
        compiler_params=_params(est, 2),
        name="nbr_attention",
    )(qkv_x, qkv_x, qkv_x, qkv_c, qkv_c, qkv_c, rp)
    return (outs[0], outs[1]) if need_ctx else (outs[0], None)


def _proj_ffn_kernel(x_ref, a_ref, mod_ref, wo_ref, g_ref, w1_ref, w3_ref, w2_ref, *rest,
                     tm, tpg, gbase, fc, final):
    final_g_ref = rest[0] if final else None
    o_ref, hid_ref = rest[-2:]
    d = x_ref.shape[-1]
    m = _mod_row(mod_ref, tm, tpg, gbase)
    y = jnp.dot(a_ref[...], wo_ref[...], preferred_element_type=F32)
    x1 = x_ref[...] + _chunk(m, 2, d) * y
    _finish(_ffn_tail(x1, m, g_ref[...], w1_ref, w3_ref, w2_ref, hid_ref, fc), final_g_ref, o_ref)


def _conv_ffn_kernel(x_ref, z_ref, zp_ref, zn_ref, gb_ref, cw_ref, mod_ref, wo_ref, g_ref,
                     w1_ref, w3_ref, w2_ref, o_ref, hid_ref, *, tm, tpg, gbase, fc, seq):
    d = x_ref.shape[-1]
    i = pl.program_id(0)
    m = _mod_row(mod_ref, tm, tpg, gbase)
    pos0 = lax.rem(i * tm, seq)
    z = z_ref[...].astype(F32)
    prev = zp_ref[...].astype(F32)[BF16_SUBLANES - 1:, :]
    nxt = zn_ref[...].astype(F32)[:1, :]
    prev = jnp.where(pos0 == 0, jnp.zeros_like(prev), prev)
    nxt = jnp.where(pos0 + tm == seq, jnp.zeros_like(nxt), nxt)
    row = lax.broadcasted_iota(jnp.int32, z.shape, 0)
    z_m1 = jnp.where(row == 0, prev, pltpu.roll(z, 1, 0))
    z_p1 = jnp.where(row == tm - 1, nxt, pltpu.roll(z, tm - 1, 0))
    cw = cw_ref[...]
    conv = z_m1 * cw[0:1] + z * cw[1:2] + z_p1 * cw[2:3]
    a = (gb_ref[...].astype(F32) * conv).astype(BF16)
    y = jnp.dot(a, wo_ref[...], preferred_element_type=F32)
    x1 = x_ref[...] + _chunk(m, 2, d) * y
    _finish(_ffn_tail(x1, m, g_ref[...], w1_ref, w3_ref, w2_ref, hid_ref, fc), None, o_ref)


POOL_PAD = 2 * F32_SUBLANES
POOL_CHUNK_ROWS = 256


def _window_sum(hext_ref, buf_a, buf_b, cols, levels, r0, rc):
    lo, hi = F32_SUBLANES, rc + POOL_PAD + F32_SUBLANES

    def pair(load, rows0, n, shift_lo, shift_hi):
        return load(rows0 - shift_lo, n) + load(rows0 + shift_hi, n)

    src = lambda e0, n: hext_ref[pl.ds(r0 + e0, n), cols]
    shift_lo, shift_hi = 1, 0
    for level in range(1, levels):
        dst = buf_a if level % 2 == 1 else buf_b
        dst[pl.ds(lo, hi - lo), :] = pair(src, lo, hi - lo, shift_lo, shift_hi)
        src = lambda e0, n, ref=dst: ref[pl.ds(e0, n), :]
        shift_lo = shift_hi = 2 ** (level - 1)
    return pair(src, POOL_PAD, rc, shift_lo, shift_hi)


def _pool_ffn_kernel(x_ref, xp_ref, xn_ref, mod_ref, gm_ref, pw_ref, ps_ref, g_ref,
                     w1_ref, w3_ref, w2_ref, o_ref, hid_ref, hext_ref, buf_a, buf_b, x1_ref,
                     *, tm, tpg, gbase, fc, seq):
    d = x_ref.shape[-1]
    halo = F32_SUBLANES
    i = pl.program_id(0)
    m = _mod_row(mod_ref, tm, tpg, gbase)
    shift, scale = _chunk(m, 0, d), _chunk(m, 1, d)
    gm = gm_ref[...]
    pos0 = lax.rem(i * tm, seq)
    h_top = _norm_mod(xp_ref[...], gm, shift, scale)
    h_bot = _norm_mod(xn_ref[...], gm, shift, scale)
    hext_ref[0:halo, :] = jnp.zeros((halo, d), F32)
    hext_ref[halo:POOL_PAD, :] = jnp.where(pos0 == 0, jnp.zeros_like(h_top), h_top)
    hext_ref[POOL_PAD:POOL_PAD + tm, :] = _norm_mod(x_ref[...], gm, shift, scale)
    hext_ref[POOL_PAD + tm:POOL_PAD + tm + halo, :] = jnp.where(pos0 + tm == seq, jnp.zeros_like(h_bot), h_bot)
    hext_ref[POOL_PAD + tm + halo:, :] = jnp.zeros((halo, d), F32)
    rc = min(tm, POOL_CHUNK_ROWS)
    for buf in (buf_a, buf_b):
        buf[0:halo, :] = jnp.zeros((halo, buf.shape[1]), F32)
        buf[POOL_PAD + rc + halo:, :] = jnp.zeros((halo, buf.shape[1]), F32)
    gc = d // len(POOL_WINDOWS)

    def pool_group(r0, gi, win):
        cols = slice(gi * gc, (gi + 1) * gc)
        pos = pos0 + r0 + lax.broadcasted_iota(jnp.int32, (rc, 1), 0)
        acc = _window_sum(hext_ref, buf_a, buf_b, cols, win.bit_length() - 1, r0, rc)
        cnt = (jnp.minimum(pos + win // 2, seq) - jnp.maximum(pos - win // 2, 0)).astype(F32)
        pooled = (acc / cnt - hext_ref[pl.ds(POOL_PAD + r0, rc), cols]).astype(BF16)
        y = jnp.dot(pooled, pw_ref[gi], preferred_element_type=F32) * ps_ref[:, cols]
        x1_ref[pl.ds(r0, rc), cols] = x_ref[pl.ds(r0, rc), cols] + m[:, 2 * d + gi * gc:2 * d + (gi + 1) * gc] * y

    def chunk_jobs(r0):
        return [functools.partial(pool_group, r0, gi, win) for gi, win in enumerate(POOL_WINDOWS)]

    for job in chunk_jobs(0):
        job()
    for r0 in range(0, tm, rc):
        nxt = chunk_jobs(r0 + rc) if r0 + rc < tm else []
        out = _ffn_tail(x1_ref[pl.ds(r0, rc), :], m, g_ref[...], w1_ref, w3_ref, w2_ref,
                        hid_ref.at[pl.ds(r0, rc), :], fc, side_jobs=nxt)
        o_ref[pl.ds(r0, rc), :] = out


def _ffn_est(tm, d, f):
    return 3 * d * f * 2 + 4 * tm * d * 4 + tm * f * 2 + 6 * tm * d * 4 + 3 * tm * 512 * 4


def _ffn_consts(g_ffn, w1, w3, w2):
    return ([g_ffn, w1, w3, w2],
            [_const_spec(g_ffn.shape), _const_spec(w1.shape), _const_spec(w3.shape), _const_spec(w2.shape)])


def _proj_ffn(x, a, mod, wo, ffn, tm, tpg, gbase, fc, final_g=None):
    t, d = x.shape
    f = ffn[1].shape[-1]
    wo, woi = wo
    tile = pl.BlockSpec((tm, d), lambda i: (i, 0))
    fargs, fspecs = _ffn_consts(*ffn)
    args = [x, a, mod, wo] + fargs
    specs = [tile, tile, _const_spec(mod.shape), _layer_spec(wo, woi)] + fspecs
    if final_g is not None:
        args.append(final_g)
        specs.append(_const_spec(final_g.shape))
    kern = functools.partial(_proj_ffn_kernel, tm=tm, tpg=tpg, gbase=gbase, fc=fc, final=final_g is not None)
    return pl.pallas_call(
        kern, grid=(t // tm,), in_specs=specs, out_specs=tile,
        out_shape=jax.ShapeDtypeStruct((t, d), F32),
        scratch_shapes=[pltpu.VMEM((tm, f), BF16)],
        compiler_params=_params(_ffn_est(tm, d, f) + d * d * 2 + 2 * tm * d * 2),
        name="proj_ffn",
    )(*args)


def _conv_ffn(x, z, gb, conv_w, mod, wo, ffn, tm, tpg, gbase, fc, seq):
    t, d = x.shape
    f = ffn[1].shape[-1]
    wo, woi = wo
    tile = pl.BlockSpec((tm, d), lambda i: (i, 0))
    hb = tm // BF16_SUBLANES
    nhb = t // BF16_SUBLANES
    prev = pl.BlockSpec((BF16_SUBLANES, d), lambda i: (jnp.maximum(i * hb - 1, 0), 0))
    nxt = pl.BlockSpec((BF16_SUBLANES, d), lambda i: (jnp.minimum((i + 1) * hb, nhb - 1), 0))
    fargs, fspecs = _ffn_consts(*ffn)
    kern = functools.partial(_conv_ffn_kernel, tm=tm, tpg=tpg, gbase=gbase, fc=fc, seq=seq)
    return pl.pallas_call(
        kern, grid=(t // tm,),
        in_specs=[tile, tile, prev, nxt, tile, _const_spec(conv_w.shape), _const_spec(mod.shape),
                  _layer_spec(wo, woi)] + fspecs,
        out_specs=tile,
        out_shape=jax.ShapeDtypeStruct((t, d), F32),
        scratch_shapes=[pltpu.VMEM((tm, f), BF16)],
        compiler_params=_params(_ffn_est(tm, d, f) + d * d * 2 + 4 * tm * d * 2 + 4 * tm * d * 4),
        name="conv_ffn",
    )(x, z, z, z, gb, conv_w, mod, wo, *fargs)


def _pool_ffn(x, mod, g_mix, pool_w, pool_scale, ffn, tm, tpg, gbase, fc, seq):
    t, d = x.shape
    f = ffn[1].shape[-1]
    pool_w, pwi = pool_w
    assert all(w & (w - 1) == 0 and w // 2 <= F32_SUBLANES for w in POOL_WINDOWS)
    tile = pl.BlockSpec((tm, d), lambda i: (i, 0))
    hb = tm // F32_SUBLANES
    nhb = t // F32_SUBLANES
    prev = pl.BlockSpec((F32_SUBLANES, d), lambda i: (jnp.maximum(i * hb - 1, 0), 0))
    nxt = pl.BlockSpec((F32_SUBLANES, d), lambda i: (jnp.minimum((i + 1) * hb, nhb - 1), 0))
    fargs, fspecs = _ffn_consts(*ffn)
    kern = functools.partial(_pool_ffn_kernel, tm=tm, tpg=tpg, gbase=gbase, fc=fc, seq=seq)
    return pl.pallas_call(
        kern, grid=(t // tm,),
        in_specs=[tile, prev, nxt, _const_spec(mod.shape), _const_spec(g_mix.shape),
                  _layer_spec(pool_w, pwi), _const_spec(pool_scale.shape)] + fspecs,
        out_specs=tile,
        out_shape=jax.ShapeDtypeStruct((t, d), F32),
        scratch_shapes=[pltpu.VMEM((tm, f), BF16), pltpu.VMEM((tm + 2 * POOL_PAD, d), F32)]
        + [pltpu.VMEM((min(tm, POOL_CHUNK_ROWS) + 2 * POOL_PAD, d // len(POOL_WINDOWS)), F32)] * 2
        + [pltpu.VMEM((tm, d), F32)],
        compiler_params=_params(_ffn_est(tm, d, f) + 4 * tm * d * 4),
        name="pool_ffn",
    )(x, x, x, mod, g_mix, pool_w, pool_scale, *fargs)


def _token_tile(seq):
    return min(seq, 1024)


def kernel(x, c, ctx, c_ctx, adaln_w, adaln_b, norm_mix_g, norm_ffn_g, ffn_w1, ffn_w3, ffn_w2,
           na_w_qkv, na_w_o, na_rpb, sc_w_in, sc_conv_w, sc_w_out, pool_w, pool_scale, final_g):
    batch, n, d = x.shape
    l = ctx.shape[1]
    depth = adaln_w.shape[0]
    f = ffn_w1.shape[-1]
    assert batch + 1 <= MOD_ROWS and d % LANES == 0 and n % GRID_W == 0
    assert d // N_HEADS * HEADS_PER_BLOCK == LANES
    fc = 256
    assert f % fc == 0
    assert (depth - 1) % N_MIXERS == 0, "the final norm is fused into the attention-layer tail"
    tm_x, tm_c = _token_tile(n), _token_tile(l)
    q_scale = float(d // N_HEADS) ** -0.5 * LOG2_E

    c_rows = jnp.concatenate([c, c_ctx[None, :], jnp.zeros((MOD_ROWS - batch - 1, d), F32)], axis=0)
    mod_all = _modulation(c_rows, adaln_w, adaln_b)

    wqkv_all, wo_all = na_w_qkv.astype(BF16), na_w_o.astype(BF16)
    win_all, wout_all = sc_w_in.astype(BF16), sc_w_out.astype(BF16)
    pw_all = pool_w.astype(BF16)
    ffn_stacks = (ffn_w1, ffn_w3, ffn_w2)
    steps_x = batch * n // tm_x
    side_ok = all(_cast_rows_per_step(st.shape[1], steps_x) is not None for st in ffn_stacks)
    cast_plan, ffn_bf16 = {}, {}
    host = None
    for layer in range(depth):
        if layer % N_MIXERS != N_MIXERS - 1 and side_ok:
            host = layer
        if host is None:
            ffn_bf16[layer] = tuple(st[layer].astype(BF16) for st in ffn_stacks)
        else:
            cast_plan.setdefault(host, []).append(layer)

    xs = x.reshape(batch * n, d)
    cs = ctx.reshape(batch * l, d)
    x_grp = dict(tm=tm_x, tpg=n, gbase=0)
    c_grp = dict(tm=tm_c, tpg=batch * l, gbase=batch)

    for layer in range(depth):
        kind = layer % N_MIXERS
        j = layer // N_MIXERS
        need_ctx = layer < depth - 1
        mod = mod_all[layer]
        g_mix = norm_mix_g[layer].reshape(1, d)
        g_ffn = norm_ffn_g[layer].reshape(1, d)
        fin = final_g.reshape(1, d) if layer == depth - 1 else None
        hosted = cast_plan.get(layer, [])
        casts = [(st, li) for li in hosted for st in ffn_stacks]

        if kind == 0:
            wqkv = (wqkv_all, j)
            wo = (wo_all, j)
            rp = _rpb_rows(na_rpb[j])
            qkv_x, cast_outs = _qkv_proj(xs, mod, g_mix, wqkv, q_scale=q_scale, casts=casts, **x_grp)
            ffn_bf16.update(_split_casts(hosted, cast_outs, len(ffn_stacks)))
            qkv_c, _ = _qkv_proj(cs, mod, g_mix, wqkv, q_scale=q_scale, **c_grp)
            ffn = (g_ffn,) + ffn_bf16[layer]
            o_x, o_c = _attention(qkv_x, qkv_c, rp, batch, need_ctx)
            xs = _proj_ffn(xs, o_x, mod, wo, ffn, fc=fc, final_g=fin, **x_grp)
            if need_ctx:
                cs = _proj_ffn(cs, o_c, mod, wo, ffn, fc=fc, **c_grp)
        elif kind == 1:
            w_in = (win_all, j)
            wo = (wout_all, j)
            gb_x, z_x, cast_outs = _conv_in_proj(xs, mod, g_mix, w_in, casts=casts, **x_grp)
            ffn_bf16.update(_split_casts(hosted, cast_outs, len(ffn_stacks)))
            ffn = (g_ffn,) + ffn_bf16[layer]
            xs_new = _conv_ffn(xs, z_x, gb_x, sc_conv_w[j], mod, wo, ffn, fc=fc, seq=n, **x_grp)
            if need_ctx:
                gb_c, z_c, _ = _conv_in_proj(cs, mod, g_mix, w_in, **c_grp)
                cs = _conv_ffn(cs, z_c, gb_c, sc_conv_w[j], mod, wo, ffn, fc=fc, seq=l, **c_grp)
            xs = xs_new
        else:
            pw = (pw_all, j)
            ps = pool_scale[j].reshape(1, d)
            ffn = (g_ffn,) + ffn_bf16[layer]
            xs_new = _pool_ffn(xs, mod, g_mix, pw, ps, ffn, fc=fc, seq=n, **x_grp)
            if need_ctx:
                cs = _pool_ffn(cs, mod, g_mix, pw, ps, ffn, fc=fc, seq=l, **c_grp)
            xs = xs_new
    return xs.reshape(batch, n, d)
```

```python
import functools

import numpy as np
import jax
import jax.numpy as jnp
from jax import lax
from jax.experimental import pallas as pl
from jax.experimental.pallas import tpu as pltpu

F32 = jnp.float32
BF16 = jnp.bfloat16

GRID_W = 64
N_MIXERS = 3
N_HEADS = 16
WIN_H = 8
WIN_W = 16
CONV_W = 3
POOL_WINDOWS = (2, 4, 8, 16)
N_MOD = 6
EPS = 1e-6
LOG2_E = 1.4426950408889634

V7X_VMEM_BYTES = 64 * 1024 * 1024
LANES = 128
F32_SUBLANES = 8
BF16_SUBLANES = 16

HEADS_PER_BLOCK = 2
MOD_ROWS = 8


def _vmem_limit(est_bytes):
    return int(min(V7X_VMEM_BYTES - 6 * 1024 * 1024, max(32 * 1024 * 1024, est_bytes * 5 // 4)))


def _params(est_bytes, ndims=1):
    return pltpu.CompilerParams(dimension_semantics=("arbitrary",) * ndims,
                                vmem_limit_bytes=_vmem_limit(est_bytes))


def _const_spec(shape):
    nd = len(shape)
    return pl.BlockSpec(shape, lambda *_: (0,) * nd, pipeline_mode=pl.Buffered(1))


def _weight_arg(w):
    if isinstance(w, tuple):
        return w[0], _layer_spec(*w)
    return w, _const_spec(w.shape)


def _layer_spec(stack, idx):
    nd = stack.ndim
    return pl.BlockSpec((None,) + stack.shape[1:], lambda *_: (idx,) + (0,) * (nd - 1),
                        pipeline_mode=pl.Buffered(1))


def _split_bf16(v):
    hi = v.astype(BF16)
    lo = (v - hi.astype(F32)).astype(BF16)
    return hi, lo


def _mod_kernel(c_ref, w_ref, b_ref, o_ref):
    c = c_ref[...]
    s = c * jax.nn.sigmoid(c)
    s_hi, s_lo = _split_bf16(s)
    lhs = jnp.concatenate([s_hi, s_lo], axis=0)
    w_hi, w_lo = _split_bf16(w_ref[...])
    r = (jnp.dot(lhs, w_hi, preferred_element_type=F32)
         + jnp.dot(lhs, w_lo, preferred_element_type=F32))
    o_ref[...] = r[:MOD_ROWS] + r[MOD_ROWS:] + b_ref[...]


def _modulation(c_rows, adaln_w, adaln_b):
    depth, d, nm = adaln_w.shape
    nc = nm // 4
    est = 2 * d * nc * 4 + 3 * d * nc * 2 + 4 * MOD_ROWS * nc * 4
    return pl.pallas_call(
        _mod_kernel,
        grid=(depth, nm // nc),
        in_specs=[
            pl.BlockSpec((MOD_ROWS, d), lambda l, j: (0, 0)),
            pl.BlockSpec((None, d, nc), lambda l, j: (l, 0, j)),
            pl.BlockSpec((None, 1, nc), lambda l, j: (l, 0, j)),
        ],
        out_specs=pl.BlockSpec((None, MOD_ROWS, nc), lambda l, j: (l, 0, j)),
        out_shape=jax.ShapeDtypeStruct((depth, MOD_ROWS, nm), F32),
        compiler_params=_params(est, 2),
        name="adaln_mod",
    )(c_rows, adaln_w, adaln_b.reshape(depth, 1, nm))


def _mod_row(mod_ref, tm, tokens_per_group, group_base):
    grp = group_base + lax.div(pl.program_id(0) * tm, tokens_per_group)
    return mod_ref[pl.ds(grp, 1), :]


def _chunk(m, idx, d):
    return m[:, idx * d:(idx + 1) * d]


def _rms(x, g):
    ms = jnp.mean(x * x, axis=-1, keepdims=True)
    return x * lax.rsqrt(ms + EPS) * g


def _norm_mod(x, g, shift, scale):
    return _rms(x, g) * (1.0 + scale) + shift


def _ffn_tail(x1, m, g_ffn, w1_ref, w3_ref, w2_ref, hid_ref, fc, side_jobs=()):
    d = x1.shape[-1]
    f = w1_ref.shape[-1]
    side_jobs = list(side_jobs)
    h = _norm_mod(x1, g_ffn, _chunk(m, 3, d), _chunk(m, 4, d)).astype(BF16)
    for c in range(f // fc):
        sl = slice(c * fc, (c + 1) * fc)
        a = jnp.dot(h, w1_ref[:, sl], preferred_element_type=F32)
        b = jnp.dot(h, w3_ref[:, sl], preferred_element_type=F32)
        hid_ref[:, sl] = (a * jax.nn.sigmoid(a) * b).astype(BF16)
        if side_jobs:
            side_jobs.pop(0)()
    while side_jobs:
        side_jobs.pop(0)()
    y = jnp.dot(hid_ref[...], w2_ref[...], preferred_element_type=F32)
    return x1 + _chunk(m, 5, d) * y


def _finish(out, final_g_ref, o_ref):
    if final_g_ref is not None:
        out = _rms(out, final_g_ref[...])
    o_ref[...] = out


def _cast_side_job(w_ref, refs, n_cast, n_out):
    wbf_ref = refs[-1]

    @pl.when(pl.program_id(0) == 0)
    def _():
        wbf_ref[...] = w_ref[...].astype(BF16)

    for src, dst in zip(refs[:n_cast], refs[n_cast + n_out:-1]):
        dst[...] = src[...].astype(BF16)
    return refs[n_cast:n_cast + n_out], wbf_ref


def _qkv_kernel(x_ref, mod_ref, g_ref, w_ref, *refs, tm, tpg, gbase, q_scale, n_cast):
    (o_ref,), w_ref = _cast_side_job(w_ref, refs, n_cast, 1)
    d = x_ref.shape[-1]
    m = _mod_row(mod_ref, tm, tpg, gbase)
    h = _norm_mod(x_ref[...], g_ref[...], _chunk(m, 0, d), _chunk(m, 1, d)).astype(BF16)
    for j in range(3):
        r = jnp.dot(h, w_ref[:, j * d:(j + 1) * d], preferred_element_type=F32)
        if j == 0:
            r = r * q_scale
        o_ref[:, j * d:(j + 1) * d] = r.astype(BF16)


def _conv_in_kernel(x_ref, mod_ref, g_ref, w_ref, *refs, tm, tpg, gbase, n_cast):
    (gb_ref, z_ref), w_ref = _cast_side_job(w_ref, refs, n_cast, 2)
    d = x_ref.shape[-1]
    m = _mod_row(mod_ref, tm, tpg, gbase)
    h = _norm_mod(x_ref[...], g_ref[...], _chunk(m, 0, d), _chunk(m, 1, d)).astype(BF16)
    gb_ref[...] = jnp.dot(h, w_ref[:, :d], preferred_element_type=F32).astype(BF16)
    gc = jnp.dot(h, w_ref[:, d:2 * d], preferred_element_type=F32)
    val = jnp.dot(h, w_ref[:, 2 * d:], preferred_element_type=F32)
    z_ref[...] = (gc * val).astype(BF16)


def _cast_rows_per_step(n_rows, n_steps):
    if n_rows % n_steps == 0 and (n_rows // n_steps) % BF16_SUBLANES == 0:
        return n_rows // n_steps
    return None


def _front_call(kernel, x, mod, g, w, out_shapes, out_specs, tm, name, casts):
    t, d = x.shape
    w, wi = w
    nout = w.shape[-1]
    steps = t // tm
    est = 2 * tm * d * 4 + d * nout * (4 + 2) + 2 * tm * nout * 2 + tm * nout * 4 + 4 * tm * d * 4
    cast_args, cast_in_specs, cast_out_specs, cast_shapes = [], [], [], []
    for stack, li in casts:
        _, r, c = stack.shape
        rb = _cast_rows_per_step(r, steps)
        cast_args.append(stack)
        cast_in_specs.append(pl.BlockSpec((None, rb, c), lambda i, li=li: (li, i, 0)))
        cast_out_specs.append(pl.BlockSpec((rb, c), lambda i: (i, 0)))
        cast_shapes.append(jax.ShapeDtypeStruct((r, c), BF16))
        est += 2 * rb * c * (4 + 2)
    outs = pl.pallas_call(
        functools.partial(kernel, n_cast=len(casts)),
        grid=(steps,),
        in_specs=[
            pl.BlockSpec((tm, d), lambda i: (i, 0)),
            _const_spec(mod.shape),
            _const_spec(g.shape),
            _layer_spec(w, wi),
        ] + cast_in_specs,
        out_specs=list(out_specs) + cast_out_specs,
        out_shape=list(out_shapes) + cast_shapes,
        scratch_shapes=[pltpu.VMEM((d, nout), BF16)],
        compiler_params=_params(est),
        name=name,
    )(x, mod, g, w, *cast_args)
    n_own = len(out_shapes)
    return outs[:n_own], outs[n_own:]


def _qkv_proj(x, mod, g, w, tm, tpg, gbase, q_scale, casts=()):
    t, d = x.shape
    kern = functools.partial(_qkv_kernel, tm=tm, tpg=tpg, gbase=gbase, q_scale=q_scale)
    (qkv,), cast = _front_call(kern, x, mod, g, w, [jax.ShapeDtypeStruct((t, 3 * d), BF16)],
                               [pl.BlockSpec((tm, 3 * d), lambda i: (i, 0))], tm, "qkv_proj", casts)
    return qkv, cast


def _conv_in_proj(x, mod, g, w, tm, tpg, gbase, casts=()):
    t, d = x.shape
    kern = functools.partial(_conv_in_kernel, tm=tm, tpg=tpg, gbase=gbase)
    spec = pl.BlockSpec((tm, d), lambda i: (i, 0))
    (gb, z), cast = _front_call(kern, x, mod, g, w, [jax.ShapeDtypeStruct((t, d), BF16)] * 2, [spec, spec],
                                tm, "conv_in_proj", casts)
    return gb, z, cast


def _split_casts(layers, cast_outs, per_layer):
    return {li: tuple(cast_outs[per_layer * k:per_layer * (k + 1)]) for k, li in enumerate(layers)}


def _stack_heads(q):
    lane = lax.broadcasted_iota(jnp.int32, q.shape, 1)
    zero = jnp.zeros_like(q)
    return jnp.concatenate([jnp.where(lane < LANES // 2, q, zero),
                            jnp.where(lane >= LANES // 2, q, zero)], axis=0)


_NT = (((1,), (1,)), ((), ()))


QROWS = 4


def _band_plan(rows, kh):
    band = QROWS + kh
    assert rows % QROWS == 0 and band % 2 == 0 and rows >= band + QROWS

    def plan(r0):
        ub = int(np.clip(r0 - kh // 2, 0, rows - band))
        tab = np.full((QROWS, band), -1, np.int64)
        for ri in range(QROWS):
            r = r0 + ri
            start = int(np.clip(r - kh // 2, 0, rows - kh))
            for u in range(band):
                if start <= ub + u < start + kh:
                    tab[ri, u] = ub + u - r + WIN_H - 1
        return ub - r0, tab

    first, last = plan(0), plan(rows - QROWS)
    interior = plan(QROWS)
    for r0 in range(QROWS, rows - QROWS, QROWS):
        off, tab = plan(r0)
        assert off == interior[0] and (tab == interior[1]).all()
    return band, [first, interior, last]


def _build_bias(rp_ref, xm_ref, bias_ref, plans):
    half = LANES // 2
    shape = (GRID_W, LANES)
    qcol = lax.broadcasted_iota(jnp.int32, shape, 0)
    lane = lax.broadcasted_iota(jnp.int32, shape, 1)
    kcol = jnp.bitwise_and(lane, half - 1)
    col_start = jnp.clip(qcol - WIN_W // 2, 0, GRID_W - WIN_W)
    in_window = (kcol >= col_start) & (kcol < col_start + WIN_W)
    low = lane < half
    neg = jnp.full(shape, -jnp.inf, F32)
    n_dr = rp_ref.shape[1]
    for hh in range(HEADS_PER_BLOCK):
        for dr in range(n_dr):
            base = jnp.broadcast_to(rp_ref[hh, dr:dr + 1, :] * LOG2_E, shape)
            t_lo = pltpu.roll(base, 0, 1, stride=1, stride_axis=0)
            both = jnp.where(low, t_lo, pltpu.roll(t_lo, half, 1))
            xm_ref[hh, dr] = jnp.where(in_window, both, neg)
    for v, (_, tab) in enumerate(plans):
        for hh in range(HEADS_PER_BLOCK):
            for ri in range(QROWS):
                r_lo = (hh * QROWS + ri) * GRID_W
                for j in range(tab.shape[1] // 2):
                    da, db = int(tab[ri, 2 * j]), int(tab[ri, 2 * j + 1])
                    ta = xm_ref[hh, da] if da >= 0 else neg
                    tb = xm_ref[hh, db] if db >= 0 else neg
                    bias_ref[v, r_lo:r_lo + GRID_W, j * LANES:(j + 1) * LANES] = jnp.where(low, ta, tb)


SOFTMAX_SLAB = 8
BLOCKS_PER_STEP = 16
SCORES_AHEAD = BLOCKS_PER_STEP


def _softmax_rows(s_parts, p_ref):
    n_rows = s_parts[0].shape[0]
    for r in range(0, n_rows, SOFTMAX_SLAB):
        rs = slice(r, r + SOFTMAX_SLAB)
        blocks = [s[rs, c:c + LANES] for s in s_parts for c in range(0, s.shape[1], LANES)]
        mx = jnp.max(functools.reduce(jnp.maximum, blocks), axis=-1, keepdims=True)
        for j, blk in enumerate(blocks):
            p_ref[rs, j * LANES:(j + 1) * LANES] = jnp.exp2(blk - mx).astype(BF16)


def _pv_heads(p_ref, n_q, pieces):
    half = LANES // 2
    outs = []
    for hh in range(HEADS_PER_BLOCK):
        hs = slice(hh * n_q, (hh + 1) * n_q)
        o = None
        for cols, v in pieces:
            own = (lax.broadcasted_iota(jnp.int32, v.shape, 1) < half) == (hh == 0)
            part = jnp.dot(p_ref[hs, cols], jnp.where(own, v, jnp.ones_like(v)), preferred_element_type=F32)
            o = part if o is None else o + part
        outs.append(o)
    low = lax.broadcasted_iota(jnp.int32, (n_q, LANES), 1) < half
    num = jnp.where(low, outs[0], outs[1])
    den = pltpu.roll(jnp.where(low, outs[1], outs[0]), half, 1)
    return num / den


def _attn_kernel(q_ref, k_ref, v_ref, qc_ref, kc_ref, vc_ref, rp_ref, *rest, rows, kh, need_ctx):
    o_ref = rest[0]
    oc_ref = rest[1] if need_ctx else None
    xm_ref, bias_ref = rest[-2 - BLOCKS_PER_STEP:-BLOCKS_PER_STEP]
    p_refs = rest[-BLOCKS_PER_STEP:]
    band, plans = _band_plan(rows, kh)
    nblk = rows // QROWS
    qn = QROWS * GRID_W
    kn = band * GRID_W
    lc = kc_ref.shape[0]

    @pl.when(pl.program_id(1) == 0)
    def _():
        _build_bias(rp_ref, xm_ref, bias_ref, plans)

    def scores(t):
        r0 = t * QROWS
        ub = jnp.clip(r0 - kh // 2, 0, rows - band)
        variant = jnp.where(t > 0, 1, 0) + jnp.where(t == nblk - 1, 1, 0)
        qoff = pl.multiple_of(r0 * GRID_W, qn)
        koff = pl.multiple_of(ub * GRID_W, GRID_W)
        qm = _stack_heads(q_ref[pl.ds(qoff, qn), :])
        s_lat = (lax.dot_general(qm, k_ref[pl.ds(koff, kn), :], _NT, preferred_element_type=F32)
                 + bias_ref[variant])
        s_ctx = lax.dot_general(qm, kc_ref[...], _NT, preferred_element_type=F32)
        return qoff, koff, s_lat, s_ctx

    def finish(blk, p_ref):
        qoff, koff, s_lat, s_ctx = blk
        _softmax_rows([s_lat, s_ctx], p_ref)
        pieces = [(slice(0, kn), v_ref[pl.ds(koff, kn), :]), (slice(kn, kn + lc), vc_ref[...])]
        o_ref[pl.ds(qoff, qn), :] = _pv_heads(p_ref, qn, pieces).astype(BF16)

    def step(tt, carry):
        pending = []
        for i in range(BLOCKS_PER_STEP):
            pending.append((scores(tt * BLOCKS_PER_STEP + i), p_refs[i]))
            if len(pending) > SCORES_AHEAD:
                finish(*pending.pop(0))
        while pending:
            finish(*pending.pop(0))
        return carry

    lax.fori_loop(0, nblk // BLOCKS_PER_STEP, step, 0)

    if need_ctx:
        s = lax.dot_general(_stack_heads(qc_ref[...]), kc_ref[...], _NT, preferred_element_type=F32)
        _softmax_rows([s], p_refs[0])
        oc_ref[...] = _pv_heads(p_refs[0], lc, [(slice(0, lc), vc_ref[...])]).astype(BF16)


def _rpb_rows(rpb):
    h, n_dr, _ = rpb.shape
    gap = jnp.zeros((h, n_dr, LANES - (2 * WIN_W - 1)), F32)
    return jnp.concatenate([rpb[..., WIN_W - 1:].astype(F32), gap, rpb[..., :WIN_W - 1].astype(F32)], axis=-1)


def _attention(qkv_x, qkv_c, rp, batch, need_ctx):
    tx, d3 = qkv_x.shape
    d = d3 // 3
    n = tx // batch
    l = qkv_c.shape[0] // batch
    rows = n // GRID_W
    kh = min(WIN_H, rows)
    npair = d // LANES
    band, plans = _band_plan(rows, kh)
    n_dr = rp.shape[1]
    kern = functools.partial(_attn_kernel, rows=rows, kh=kh, need_ctx=need_ctx)
    in_specs = [
        pl.BlockSpec((n, LANES), lambda p, b: (b, p)),
        pl.BlockSpec((n, LANES), lambda p, b: (b, npair + p)),
        pl.BlockSpec((n, LANES), lambda p, b: (b, 2 * npair + p)),
        pl.BlockSpec((l, LANES), lambda p, b: (b, p)),
        pl.BlockSpec((l, LANES), lambda p, b: (b, npair + p)),
        pl.BlockSpec((l, LANES), lambda p, b: (b, 2 * npair + p)),
        pl.BlockSpec((HEADS_PER_BLOCK, n_dr, LANES), lambda p, b: (p, 0, 0)),
    ]
    out_shape = [jax.ShapeDtypeStruct((tx, d), BF16)]
    out_specs = [pl.BlockSpec((n, LANES), lambda p, b: (b, p))]
    if need_ctx:
        out_shape.append(jax.ShapeDtypeStruct((batch * l, d), BF16))
        out_specs.append(pl.BlockSpec((l, LANES), lambda p, b: (b, p)))
    qrows = HEADS_PER_BLOCK * QROWS * GRID_W
    assert (rows // QROWS) % BLOCKS_PER_STEP == 0 and HEADS_PER_BLOCK * l <= qrows
    bias_bytes = len(plans) * qrows * band * GRID_W * 4
    est = (2 * (4 * n * LANES * 2 + 4 * l * LANES * 2) + bias_bytes
           + HEADS_PER_BLOCK * n_dr * GRID_W * LANES * 4
           + BLOCKS_PER_STEP * qrows * (band * GRID_W + l) * (2 + 4))
    outs = pl.pallas_call(
        kern,
        grid=(npair, batch),
        in_specs=in_specs,
        out_specs=out_specs,
        out_shape=out_shape,
        scratch_shapes=[pltpu.VMEM((HEADS_PER_BLOCK, n_dr, GRID_W, LANES), F32),
                        pltpu.VMEM((len(plans), qrows, band * GRID_W), F32)]
        + [pltpu.VMEM((qrows, band * GRID_W + l), BF16)] * BLOCKS_PER_STEP,
        compiler_params=_params(est, 2),
        name="nbr_attention",
    )(qkv_x, qkv_x, qkv_x, qkv_c, qkv_c, qkv_c, rp)
    return (outs[0], outs[1]) if need_ctx else (outs[0], None)


def _proj_ffn_kernel(x_ref, a_ref, mod_ref, wo_ref, g_ref, w1_ref, w3_ref, w2_ref, *rest,
                     tm, tpg, gbase, fc, final):
    final_g_ref = rest[0] if final else None
    o_ref, hid_ref = rest[-2:]
    d = x_ref.shape[-1]
    m = _mod_row(mod_ref, tm, tpg, gbase)
    y = jnp.dot(a_ref[...], wo_ref[...], preferred_element_type=F32)
    x1 = x_ref[...] + _chunk(m, 2, d) * y
    _finish(_ffn_tail(x1, m, g_ref[...], w1_ref, w3_ref, w2_ref, hid_ref, fc), final_g_ref, o_ref)


def _conv_ffn_kernel(x_ref, z_ref, zp_ref, zn_ref, gb_ref, cw_ref, mod_ref, wo_ref, g_ref,
                     w1_ref, w3_ref, w2_ref, o_ref, hid_ref, *, tm, tpg, gbase, fc, seq):
    d = x_ref.shape[-1]
    i = pl.program_id(0)
    m = _mod_row(mod_ref, tm, tpg, gbase)
    pos0 = lax.rem(i * tm, seq)
    z = z_ref[...].astype(F32)
    prev = zp_ref[...].astype(F32)[BF16_SUBLANES - 1:, :]
    nxt = zn_ref[...].astype(F32)[:1, :]
    prev = jnp.where(pos0 == 0, jnp.zeros_like(prev), prev)
    nxt = jnp.where(pos0 + tm == seq, jnp.zeros_like(nxt), nxt)
    row = lax.broadcasted_iota(jnp.int32, z.shape, 0)
    z_m1 = jnp.where(row == 0, prev, pltpu.roll(z, 1, 0))
    z_p1 = jnp.where(row == tm - 1, nxt, pltpu.roll(z, tm - 1, 0))
    cw = cw_ref[...]
    conv = z_m1 * cw[0:1] + z * cw[1:2] + z_p1 * cw[2:3]
    a = (gb_ref[...].astype(F32) * conv).astype(BF16)
    y = jnp.dot(a, wo_ref[...], preferred_element_type=F32)
    x1 = x_ref[...] + _chunk(m, 2, d) * y
    _finish(_ffn_tail(x1, m, g_ref[...], w1_ref, w3_ref, w2_ref, hid_ref, fc), None, o_ref)


POOL_PAD = 2 * F32_SUBLANES
POOL_CHUNK_ROWS = 256


def _window_sum(hext_ref, buf_a, buf_b, cols, levels, r0, rc):
    lo, hi = F32_SUBLANES, rc + POOL_PAD + F32_SUBLANES

    def pair(load, rows0, n, shift_lo, shift_hi):
        return load(rows0 - shift_lo, n) + load(rows0 + shift_hi, n)

    src = lambda e0, n: hext_ref[pl.ds(r0 + e0, n), cols]
    shift_lo, shift_hi = 1, 0
    for level in range(1, levels):
        dst = buf_a if level % 2 == 1 else buf_b
        dst[pl.ds(lo, hi - lo), :] = pair(src, lo, hi - lo, shift_lo, shift_hi)
        src = lambda e0, n, ref=dst: ref[pl.ds(e0, n), :]
        shift_lo = shift_hi = 2 ** (level - 1)
    return pair(src, POOL_PAD, rc, shift_lo, shift_hi)


def _pool_ffn_kernel(x_ref, xp_ref, xn_ref, mod_ref, gm_ref, pw_ref, ps_ref, g_ref,
                     w1_ref, w3_ref, w2_ref, o_ref, hid_ref, hext_ref, buf_a, buf_b, x1_ref,
                     *, tm, tpg, gbase, fc, seq):
    d = x_ref.shape[-1]
    halo = F32_SUBLANES
    i = pl.program_id(0)
    m = _mod_row(mod_ref, tm, tpg, gbase)
    shift, scale = _chunk(m, 0, d), _chunk(m, 1, d)
    gm = gm_ref[...]
    pos0 = lax.rem(i * tm, seq)
    h_top = _norm_mod(xp_ref[...], gm, shift, scale)
    h_bot = _norm_mod(xn_ref[...], gm, shift, scale)
    hext_ref[0:halo, :] = jnp.zeros((halo, d), F32)
    hext_ref[halo:POOL_PAD, :] = jnp.where(pos0 == 0, jnp.zeros_like(h_top), h_top)
    hext_ref[POOL_PAD:POOL_PAD + tm, :] = _norm_mod(x_ref[...], gm, shift, scale)
    hext_ref[POOL_PAD + tm:POOL_PAD + tm + halo, :] = jnp.where(pos0 + tm == seq, jnp.zeros_like(h_bot), h_bot)
    hext_ref[POOL_PAD + tm + halo:, :] = jnp.zeros((halo, d), F32)
    rc = min(tm, POOL_CHUNK_ROWS)
    for buf in (buf_a, buf_b):
        buf[0:halo, :] = jnp.zeros((halo, buf.shape[1]), F32)
        buf[POOL_PAD + rc + halo:, :] = jnp.zeros((halo, buf.shape[1]), F32)
    gc = d // len(POOL_WINDOWS)

    def pool_group(r0, gi, win):
        cols = slice(gi * gc, (gi + 1) * gc)
        pos = pos0 + r0 + lax.broadcasted_iota(jnp.int32, (rc, 1), 0)
        acc = _window_sum(hext_ref, buf_a, buf_b, cols, win.bit_length() - 1, r0, rc)
        cnt = (jnp.minimum(pos + win // 2, seq) - jnp.maximum(pos - win // 2, 0)).astype(F32)
        pooled = (acc / cnt - hext_ref[pl.ds(POOL_PAD + r0, rc), cols]).astype(BF16)
        y = jnp.dot(pooled, pw_ref[gi], preferred_element_type=F32) * ps_ref[:, cols]
        x1_ref[pl.ds(r0, rc), cols] = x_ref[pl.ds(r0, rc), cols] + m[:, 2 * d + gi * gc:2 * d + (gi + 1) * gc] * y

    def chunk_jobs(r0):
        return [functools.partial(pool_group, r0, gi, win) for gi, win in enumerate(POOL_WINDOWS)]

    for job in chunk_jobs(0):
        job()
    for r0 in range(0, tm, rc):
        nxt = chunk_jobs(r0 + rc) if r0 + rc < tm else []
        out = _ffn_tail(x1_ref[pl.ds(r0, rc), :], m, g_ref[...], w1_ref, w3_ref, w2_ref,
                        hid_ref.at[pl.ds(r0, rc), :], fc, side_jobs=nxt)
        o_ref[pl.ds(r0, rc), :] = out


def _ffn_est(tm, d, f):
    return 3 * d * f * 2 + 4 * tm * d * 4 + tm * f * 2 + 6 * tm * d * 4 + 3 * tm * 512 * 4


def _ffn_consts(g_ffn, w1, w3, w2):
    return ([g_ffn, w1, w3, w2],
            [_const_spec(g_ffn.shape), _const_spec(w1.shape), _const_spec(w3.shape), _const_spec(w2.shape)])


def _proj_ffn(x, a, mod, wo, ffn, tm, tpg, gbase, fc, final_g=None):
    t, d = x.shape
    f = ffn[1].shape[-1]
    wo, wo_spec = _weight_arg(wo)
    tile = pl.BlockSpec((tm, d), lambda i: (i, 0))
    fargs, fspecs = _ffn_consts(*ffn)
    args = [x, a, mod, wo] + fargs
    specs = [tile, tile, _const_spec(mod.shape), wo_spec] + fspecs
    if final_g is not None:
        args.append(final_g)
        specs.append(_const_spec(final_g.shape))
    kern = functools.partial(_proj_ffn_kernel, tm=tm, tpg=tpg, gbase=gbase, fc=fc, final=final_g is not None)
    return pl.pallas_call(
        kern, grid=(t // tm,), in_specs=specs, out_specs=tile,
        out_shape=jax.ShapeDtypeStruct((t, d), F32),
        scratch_shapes=[pltpu.VMEM((tm, f), BF16)],
        compiler_params=_params(_ffn_est(tm, d, f) + d * d * 2 + 2 * tm * d * 2),
        name="proj_ffn",
    )(*args)


def _conv_ffn(x, z, gb, conv_w, mod, wo, ffn, tm, tpg, gbase, fc, seq):
    t, d = x.shape
    f = ffn[1].shape[-1]
    wo, wo_spec = _weight_arg(wo)
    tile = pl.BlockSpec((tm, d), lambda i: (i, 0))
    hb = tm // BF16_SUBLANES
    nhb = t // BF16_SUBLANES
    prev = pl.BlockSpec((BF16_SUBLANES, d), lambda i: (jnp.maximum(i * hb - 1, 0), 0))
    nxt = pl.BlockSpec((BF16_SUBLANES, d), lambda i: (jnp.minimum((i + 1) * hb, nhb - 1), 0))
    fargs, fspecs = _ffn_consts(*ffn)
    kern = functools.partial(_conv_ffn_kernel, tm=tm, tpg=tpg, gbase=gbase, fc=fc, seq=seq)
    return pl.pallas_call(
        kern, grid=(t // tm,),
        in_specs=[tile, tile, prev, nxt, tile, _const_spec(conv_w.shape), _const_spec(mod.shape),
                  wo_spec] + fspecs,
        out_specs=tile,
        out_shape=jax.ShapeDtypeStruct((t, d), F32),
        scratch_shapes=[pltpu.VMEM((tm, f), BF16)],
        compiler_params=_params(_ffn_est(tm, d, f) + d * d * 2 + 4 * tm * d * 2 + 4 * tm * d * 4),
        name="conv_ffn",
    )(x, z, z, z, gb, conv_w, mod, wo, *fargs)


def _pool_ffn(x, mod, g_mix, pool_w, pool_scale, ffn, tm, tpg, gbase, fc, seq):
    t, d = x.shape
    f = ffn[1].shape[-1]
    pool_w, pw_spec = _weight_arg(pool_w)
    assert all(w & (w - 1) == 0 and w // 2 <= F32_SUBLANES for w in POOL_WINDOWS)
    tile = pl.BlockSpec((tm, d), lambda i: (i, 0))
    hb = tm // F32_SUBLANES
    nhb = t // F32_SUBLANES
    prev = pl.BlockSpec((F32_SUBLANES, d), lambda i: (jnp.maximum(i * hb - 1, 0), 0))
    nxt = pl.BlockSpec((F32_SUBLANES, d), lambda i: (jnp.minimum((i + 1) * hb, nhb - 1), 0))
    fargs, fspecs = _ffn_consts(*ffn)
    kern = functools.partial(_pool_ffn_kernel, tm=tm, tpg=tpg, gbase=gbase, fc=fc, seq=seq)
    return pl.pallas_call(
        kern, grid=(t // tm,),
        in_specs=[tile, prev, nxt, _const_spec(mod.shape), _const_spec(g_mix.shape),
                  pw_spec, _const_spec(pool_scale.shape)] + fspecs,
        out_specs=tile,
        out_shape=jax.ShapeDtypeStruct((t, d), F32),
        scratch_shapes=[pltpu.VMEM((tm, f), BF16), pltpu.VMEM((tm + 2 * POOL_PAD, d), F32)]
        + [pltpu.VMEM((min(tm, POOL_CHUNK_ROWS) + 2 * POOL_PAD, d // len(POOL_WINDOWS)), F32)] * 2
        + [pltpu.VMEM((tm, d), F32)],
        compiler_params=_params(_ffn_est(tm, d, f) + 4 * tm * d * 4),
        name="pool_ffn",
    )(x, x, x, mod, g_mix, pool_w, pool_scale, *fargs)


def _token_tile(seq):
    return min(seq, 1024)


def kernel(x, c, ctx, c_ctx, adaln_w, adaln_b, norm_mix_g, norm_ffn_g, ffn_w1, ffn_w3, ffn_w2,
           na_w_qkv, na_w_o, na_rpb, sc_w_in, sc_conv_w, sc_w_out, pool_w, pool_scale, final_g):
    batch, n, d = x.shape
    l = ctx.shape[1]
    depth = adaln_w.shape[0]
    f = ffn_w1.shape[-1]
    assert batch + 1 <= MOD_ROWS and d % LANES == 0 and n % GRID_W == 0
    assert d // N_HEADS * HEADS_PER_BLOCK == LANES
    fc = 256
    assert f % fc == 0
    assert (depth - 1) % N_MIXERS == 0, "the final norm is fused into the attention-layer tail"
    tm_x, tm_c = _token_tile(n), _token_tile(l)
    q_scale = float(d // N_HEADS) ** -0.5 * LOG2_E

    c_rows = jnp.concatenate([c, c_ctx[None, :], jnp.zeros((MOD_ROWS - batch - 1, d), F32)], axis=0)
    mod_all = _modulation(c_rows, adaln_w, adaln_b)

    gcw = d // len(POOL_WINDOWS)
    pool_rows = pool_w.reshape(pool_w.shape[0], len(POOL_WINDOWS) * gcw, gcw)
    tail_stacks = {0: na_w_o, 1: sc_w_out, 2: pool_rows}

    def tail_weights(layer):
        return [(ffn_w1, layer), (ffn_w3, layer), (ffn_w2, layer),
                (tail_stacks[layer % N_MIXERS], layer // N_MIXERS)]

    steps_x = batch * n // tm_x
    side_ok = all(_cast_rows_per_step(st.shape[1], steps_x) is not None
                  for layer in range(depth) for st, _ in tail_weights(layer))
    cast_plan, tail_bf16 = {}, {}
    host = None
    for layer in range(depth):
        if layer % N_MIXERS != N_MIXERS - 1 and side_ok:
            host = layer
        if host is None:
            tail_bf16[layer] = tuple(st[li].astype(BF16) for st, li in tail_weights(layer))
        else:
            cast_plan.setdefault(host, []).append(layer)

    xs = x.reshape(batch * n, d)
    cs = ctx.reshape(batch * l, d)
    x_grp = dict(tm=tm_x, tpg=n, gbase=0)
    c_grp = dict(tm=tm_c, tpg=batch * l, gbase=batch)

    for layer in range(depth):
        kind = layer % N_MIXERS
        j = layer // N_MIXERS
        need_ctx = layer < depth - 1
        mod = mod_all[layer]
        g_mix = norm_mix_g[layer].reshape(1, d)
        g_ffn = norm_ffn_g[layer].reshape(1, d)
        fin = final_g.reshape(1, d) if layer == depth - 1 else None
        hosted = cast_plan.get(layer, [])
        casts = [w for li in hosted for w in tail_weights(li)]
        n_tail = len(tail_weights(layer))

        if kind == 0:
            wqkv = (na_w_qkv, j)
            rp = _rpb_rows(na_rpb[j])
            qkv_x, cast_outs = _qkv_proj(xs, mod, g_mix, wqkv, q_scale=q_scale, casts=casts, **x_grp)
            tail_bf16.update(_split_casts(hosted, cast_outs, n_tail))
            qkv_c, _ = _qkv_proj(cs, mod, g_mix, wqkv, q_scale=q_scale, **c_grp)
            ffn = (g_ffn,) + tail_bf16[layer][:3]
            wo = tail_bf16[layer][3]
            o_x, o_c = _attention(qkv_x, qkv_c, rp, batch, need_ctx)
            xs = _proj_ffn(xs, o_x, mod, wo, ffn, fc=fc, final_g=fin, **x_grp)
            if need_ctx:
                cs = _proj_ffn(cs, o_c, mod, wo, ffn, fc=fc, **c_grp)
        elif kind == 1:
            w_in = (sc_w_in, j)
            gb_x, z_x, cast_outs = _conv_in_proj(xs, mod, g_mix, w_in, casts=casts, **x_grp)
            tail_bf16.update(_split_casts(hosted, cast_outs, n_tail))
            ffn = (g_ffn,) + tail_bf16[layer][:3]
            wo = tail_bf16[layer][3]
            xs_new = _conv_ffn(xs, z_x, gb_x, sc_conv_w[j], mod, wo, ffn, fc=fc, seq=n, **x_grp)
            if need_ctx:
                gb_c, z_c, _ = _conv_in_proj(cs, mod, g_mix, w_in, **c_grp)
                cs = _conv_ffn(cs, z_c, gb_c, sc_conv_w[j], mod, wo, ffn, fc=fc, seq=l, **c_grp)
            xs = xs_new
        else:
            pw = tail_bf16[layer][3].reshape(len(POOL_WINDOWS), gcw, gcw)
            ps = pool_scale[j].reshape(1, d)
            ffn = (g_ffn,) + tail_bf16[layer][:3]
            xs_new = _pool_ffn(xs, mod, g_mix, pw, ps, ffn, fc=fc, seq=n, **x_grp)
            if need_ctx:
                cs = _pool_ffn(cs, mod, g_mix, pw, ps, ffn, fc=fc, seq=l, **c_grp)
            xs = xs_new
    return xs.reshape(batch, n, d)
```

```python
import functools

import numpy as np
import jax
import jax.numpy as jnp
from jax import lax
from jax.experimental import pallas as pl
from jax.experimental.pallas import tpu as pltpu

F32 = jnp.float32
BF16 = jnp.bfloat16

GRID_W = 64
N_MIXERS = 3
N_HEADS = 16
WIN_H = 8
WIN_W = 16
CONV_W = 3
POOL_WINDOWS = (2, 4, 8, 16)
N_MOD = 6
EPS = 1e-6
LOG2_E = 1.4426950408889634

V7X_VMEM_BYTES = 64 * 1024 * 1024
LANES = 128
F32_SUBLANES = 8
BF16_SUBLANES = 16

HEADS_PER_BLOCK = 2
MOD_ROWS = 8


def _vmem_limit(est_bytes):
    return int(min(V7X_VMEM_BYTES - 6 * 1024 * 1024, max(32 * 1024 * 1024, est_bytes * 5 // 4)))


def _params(est_bytes, ndims=1):
    return pltpu.CompilerParams(dimension_semantics=("arbitrary",) * ndims,
                                vmem_limit_bytes=_vmem_limit(est_bytes))


def _const_spec(shape):
    nd = len(shape)
    return pl.BlockSpec(shape, lambda *_: (0,) * nd, pipeline_mode=pl.Buffered(1))


def _weight_arg(w):
    if isinstance(w, tuple):
        return w[0], _layer_spec(*w)
    return w, _const_spec(w.shape)


def _layer_spec(stack, idx):
    nd = stack.ndim
    return pl.BlockSpec((None,) + stack.shape[1:], lambda *_: (idx,) + (0,) * (nd - 1),
                        pipeline_mode=pl.Buffered(1))


def _split_bf16(v):
    hi = v.astype(BF16)
    lo = (v - hi.astype(F32)).astype(BF16)
    return hi, lo


def _mod_kernel(c_ref, w_ref, b_ref, o_ref):
    c = c_ref[...]
    s = c * jax.nn.sigmoid(c)
    s_hi, s_lo = _split_bf16(s)
    lhs = jnp.concatenate([s_hi, s_lo], axis=0)
    w_hi, w_lo = _split_bf16(w_ref[...])
    r = (jnp.dot(lhs, w_hi, preferred_element_type=F32)
         + jnp.dot(lhs, w_lo, preferred_element_type=F32))
    o_ref[...] = r[:MOD_ROWS] + r[MOD_ROWS:] + b_ref[...]


def _modulation(c_rows, adaln_w, adaln_b):
    depth, d, nm = adaln_w.shape
    nc = nm // 4
    est = 2 * d * nc * 4 + 3 * d * nc * 2 + 4 * MOD_ROWS * nc * 4
    return pl.pallas_call(
        _mod_kernel,
        grid=(depth, nm // nc),
        in_specs=[
            pl.BlockSpec((MOD_ROWS, d), lambda l, j: (0, 0)),
            pl.BlockSpec((None, d, nc), lambda l, j: (l, 0, j)),
            pl.BlockSpec((None, 1, nc), lambda l, j: (l, 0, j)),
        ],
        out_specs=pl.BlockSpec((None, MOD_ROWS, nc), lambda l, j: (l, 0, j)),
        out_shape=jax.ShapeDtypeStruct((depth, MOD_ROWS, nm), F32),
        compiler_params=_params(est, 2),
        name="adaln_mod",
    )(c_rows, adaln_w, adaln_b.reshape(depth, 1, nm))


def _mod_row(mod_ref, tm, tokens_per_group, group_base):
    grp = group_base + lax.div(pl.program_id(0) * tm, tokens_per_group)
    return mod_ref[pl.ds(grp, 1), :]


def _chunk(m, idx, d):
    return m[:, idx * d:(idx + 1) * d]


def _rms(x, g):
    ms = jnp.mean(x * x, axis=-1, keepdims=True)
    return x * lax.rsqrt(ms + EPS) * g


def _norm_mod(x, g, shift, scale):
    return _rms(x, g) * (1.0 + scale) + shift


def _ffn_tail(x1, m, g_ffn, w1_ref, w3_ref, w2_ref, hid_ref, fc, side_jobs=()):
    d = x1.shape[-1]
    f = w1_ref.shape[-1]
    side_jobs = list(side_jobs)
    h = _norm_mod(x1, g_ffn, _chunk(m, 3, d), _chunk(m, 4, d)).astype(BF16)
    for c in range(f // fc):
        sl = slice(c * fc, (c + 1) * fc)
        a = jnp.dot(h, w1_ref[:, sl], preferred_element_type=F32)
        b = jnp.dot(h, w3_ref[:, sl], preferred_element_type=F32)
        hid_ref[:, sl] = (a * jax.nn.sigmoid(a) * b).astype(BF16)
        if side_jobs:
            side_jobs.pop(0)()
    while side_jobs:
        side_jobs.pop(0)()
    y = jnp.dot(hid_ref[...], w2_ref[...], preferred_element_type=F32)
    return x1 + _chunk(m, 5, d) * y


def _finish(out, final_g_ref, o_ref):
    if final_g_ref is not None:
        out = _rms(out, final_g_ref[...])
    o_ref[...] = out


def _cast_side_job(w_ref, refs, n_cast, n_out):
    wbf_ref = refs[-1]

    @pl.when(pl.program_id(0) == 0)
    def _():
        wbf_ref[...] = w_ref[...].astype(BF16)

    for src, dst in zip(refs[:n_cast], refs[n_cast + n_out:-1]):
        dst[...] = src[...].astype(BF16)
    return refs[n_cast:n_cast + n_out], wbf_ref


def _qkv_kernel(x_ref, mod_ref, g_ref, w_ref, *refs, tm, tpg, gbase, q_scale, n_cast):
    (o_ref,), w_ref = _cast_side_job(w_ref, refs, n_cast, 1)
    d = x_ref.shape[-1]
    m = _mod_row(mod_ref, tm, tpg, gbase)
    h = _norm_mod(x_ref[...], g_ref[...], _chunk(m, 0, d), _chunk(m, 1, d)).astype(BF16)
    for j in range(3):
        r = jnp.dot(h, w_ref[:, j * d:(j + 1) * d], preferred_element_type=F32)
        if j == 0:
            r = r * q_scale
        o_ref[:, j * d:(j + 1) * d] = r.astype(BF16)


def _conv_in_kernel(x_ref, mod_ref, g_ref, w_ref, *refs, tm, tpg, gbase, n_cast):
    (gb_ref, z_ref), w_ref = _cast_side_job(w_ref, refs, n_cast, 2)
    d = x_ref.shape[-1]
    m = _mod_row(mod_ref, tm, tpg, gbase)
    h = _norm_mod(x_ref[...], g_ref[...], _chunk(m, 0, d), _chunk(m, 1, d)).astype(BF16)
    gb_ref[...] = jnp.dot(h, w_ref[:, :d], preferred_element_type=F32).astype(BF16)
    gc = jnp.dot(h, w_ref[:, d:2 * d], preferred_element_type=F32)
    val = jnp.dot(h, w_ref[:, 2 * d:], preferred_element_type=F32)
    z_ref[...] = (gc * val).astype(BF16)


def _cast_rows_per_step(n_rows, n_steps):
    if n_rows % n_steps == 0 and (n_rows // n_steps) % BF16_SUBLANES == 0:
        return n_rows // n_steps
    return None


def _front_call(kernel, x, mod, g, w, out_shapes, out_specs, tm, name, casts):
    t, d = x.shape
    w, wi = w
    nout = w.shape[-1]
    steps = t // tm
    est = 2 * tm * d * 4 + d * nout * (4 + 2) + 2 * tm * nout * 2 + tm * nout * 4 + 4 * tm * d * 4
    cast_args, cast_in_specs, cast_out_specs, cast_shapes = [], [], [], []
    for stack, li in casts:
        _, r, c = stack.shape
        rb = _cast_rows_per_step(r, steps)
        cast_args.append(stack)
        cast_in_specs.append(pl.BlockSpec((None, rb, c), lambda i, li=li: (li, i, 0)))
        cast_out_specs.append(pl.BlockSpec((rb, c), lambda i: (i, 0)))
        cast_shapes.append(jax.ShapeDtypeStruct((r, c), BF16))
        est += 2 * rb * c * (4 + 2)
    outs = pl.pallas_call(
        functools.partial(kernel, n_cast=len(casts)),
        grid=(steps,),
        in_specs=[
            pl.BlockSpec((tm, d), lambda i: (i, 0)),
            _const_spec(mod.shape),
            _const_spec(g.shape),
            _layer_spec(w, wi),
        ] + cast_in_specs,
        out_specs=list(out_specs) + cast_out_specs,
        out_shape=list(out_shapes) + cast_shapes,
        scratch_shapes=[pltpu.VMEM((d, nout), BF16)],
        compiler_params=_params(est),
        name=name,
    )(x, mod, g, w, *cast_args)
    n_own = len(out_shapes)
    return outs[:n_own], outs[n_own:]


def _qkv_proj(x, mod, g, w, tm, tpg, gbase, q_scale, casts=()):
    t, d = x.shape
    kern = functools.partial(_qkv_kernel, tm=tm, tpg=tpg, gbase=gbase, q_scale=q_scale)
    (qkv,), cast = _front_call(kern, x, mod, g, w, [jax.ShapeDtypeStruct((t, 3 * d), BF16)],
                               [pl.BlockSpec((tm, 3 * d), lambda i: (i, 0))], tm, "qkv_proj", casts)
    return qkv, cast


def _conv_in_proj(x, mod, g, w, tm, tpg, gbase, casts=()):
    t, d = x.shape
    kern = functools.partial(_conv_in_kernel, tm=tm, tpg=tpg, gbase=gbase)
    spec = pl.BlockSpec((tm, d), lambda i: (i, 0))
    (gb, z), cast = _front_call(kern, x, mod, g, w, [jax.ShapeDtypeStruct((t, d), BF16)] * 2, [spec, spec],
                                tm, "conv_in_proj", casts)
    return gb, z, cast


def _split_casts(layers, cast_outs, per_layer):
    return {li: tuple(cast_outs[per_layer * k:per_layer * (k + 1)]) for k, li in enumerate(layers)}


def _stack_heads(q):
    lane = lax.broadcasted_iota(jnp.int32, q.shape, 1)
    zero = jnp.zeros_like(q)
    return jnp.concatenate([jnp.where(lane < LANES // 2, q, zero),
                            jnp.where(lane >= LANES // 2, q, zero)], axis=0)


_NT = (((1,), (1,)), ((), ()))


QROWS = 4


def _band_plan(rows, kh):
    band = QROWS + kh
    assert rows % QROWS == 0 and band % 2 == 0 and rows >= band + QROWS

    def plan(r0):
        ub = int(np.clip(r0 - kh // 2, 0, rows - band))
        tab = np.full((QROWS, band), -1, np.int64)
        for ri in range(QROWS):
            r = r0 + ri
            start = int(np.clip(r - kh // 2, 0, rows - kh))
            for u in range(band):
                if start <= ub + u < start + kh:
                    tab[ri, u] = ub + u - r + WIN_H - 1
        return ub - r0, tab

    first, last = plan(0), plan(rows - QROWS)
    interior = plan(QROWS)
    for r0 in range(QROWS, rows - QROWS, QROWS):
        off, tab = plan(r0)
        assert off == interior[0] and (tab == interior[1]).all()
    return band, [first, interior, last]


def _build_bias(rp_ref, xm_ref, bias_ref, plans):
    half = LANES // 2
    shape = (GRID_W, LANES)
    qcol = lax.broadcasted_iota(jnp.int32, shape, 0)
    lane = lax.broadcasted_iota(jnp.int32, shape, 1)
    kcol = jnp.bitwise_and(lane, half - 1)
    col_start = jnp.clip(qcol - WIN_W // 2, 0, GRID_W - WIN_W)
    in_window = (kcol >= col_start) & (kcol < col_start + WIN_W)
    low = lane < half
    neg = jnp.full(shape, -jnp.inf, F32)
    n_dr = rp_ref.shape[1]
    for hh in range(HEADS_PER_BLOCK):
        for dr in range(n_dr):
            base = jnp.broadcast_to(rp_ref[hh, dr:dr + 1, :] * LOG2_E, shape)
            t_lo = pltpu.roll(base, 0, 1, stride=1, stride_axis=0)
            both = jnp.where(low, t_lo, pltpu.roll(t_lo, half, 1))
            xm_ref[hh, dr] = jnp.where(in_window, both, neg)
    for v, (_, tab) in enumerate(plans):
        for hh in range(HEADS_PER_BLOCK):
            for ri in range(QROWS):
                r_lo = (hh * QROWS + ri) * GRID_W
                for j in range(tab.shape[1] // 2):
                    da, db = int(tab[ri, 2 * j]), int(tab[ri, 2 * j + 1])
                    ta = xm_ref[hh, da] if da >= 0 else neg
                    tb = xm_ref[hh, db] if db >= 0 else neg
                    bias_ref[v, r_lo:r_lo + GRID_W, j * LANES:(j + 1) * LANES] = jnp.where(low, ta, tb)


SOFTMAX_SLAB = 8
BLOCKS_PER_STEP = 16
SCORES_AHEAD = BLOCKS_PER_STEP


def _softmax_rows(s_parts, p_ref):
    n_rows = s_parts[0].shape[0]
    for r in range(0, n_rows, SOFTMAX_SLAB):
        rs = slice(r, r + SOFTMAX_SLAB)
        blocks = [s[rs, c:c + LANES] for s in s_parts for c in range(0, s.shape[1], LANES)]
        mx = jnp.max(functools.reduce(jnp.maximum, blocks), axis=-1, keepdims=True)
        for j, blk in enumerate(blocks):
            p_ref[rs, j * LANES:(j + 1) * LANES] = jnp.exp2(blk - mx).astype(BF16)


def _pv_heads(p_ref, n_q, pieces):
    half = LANES // 2
    outs = []
    for hh in range(HEADS_PER_BLOCK):
        hs = slice(hh * n_q, (hh + 1) * n_q)
        o = None
        for cols, v in pieces:
            own = (lax.broadcasted_iota(jnp.int32, v.shape, 1) < half) == (hh == 0)
            part = jnp.dot(p_ref[hs, cols], jnp.where(own, v, jnp.ones_like(v)), preferred_element_type=F32)
            o = part if o is None else o + part
        outs.append(o)
    low = lax.broadcasted_iota(jnp.int32, (n_q, LANES), 1) < half
    num = jnp.where(low, outs[0], outs[1])
    den = pltpu.roll(jnp.where(low, outs[1], outs[0]), half, 1)
    return num / den


def _attn_kernel(q_ref, k_ref, v_ref, qc_ref, kc_ref, vc_ref, rp_ref, *rest, rows, kh, need_ctx):
    o_ref = rest[0]
    oc_ref = rest[1] if need_ctx else None
    xm_ref, bias_ref = rest[-2 - BLOCKS_PER_STEP:-BLOCKS_PER_STEP]
    p_refs = rest[-BLOCKS_PER_STEP:]
    band, plans = _band_plan(rows, kh)
    nblk = rows // QROWS
    qn = QROWS * GRID_W
    kn = band * GRID_W
    lc = kc_ref.shape[0]

    @pl.when(pl.program_id(1) == 0)
    def _():
        _build_bias(rp_ref, xm_ref, bias_ref, plans)

    def scores(t):
        r0 = t * QROWS
        ub = jnp.clip(r0 - kh // 2, 0, rows - band)
        variant = jnp.where(t > 0, 1, 0) + jnp.where(t == nblk - 1, 1, 0)
        qoff = pl.multiple_of(r0 * GRID_W, qn)
        koff = pl.multiple_of(ub * GRID_W, GRID_W)
        qm = _stack_heads(q_ref[pl.ds(qoff, qn), :])
        kcat = jnp.concatenate([k_ref[pl.ds(koff, kn), :], kc_ref[...]], axis=0)
        s = lax.dot_general(qm, kcat, _NT, preferred_element_type=F32)
        return qoff, koff, s[:, :kn] + bias_ref[variant], s[:, kn:]

    def finish(blk, p_ref):
        qoff, koff, s_lat, s_ctx = blk
        _softmax_rows([s_lat, s_ctx], p_ref)
        vcat = jnp.concatenate([v_ref[pl.ds(koff, kn), :], vc_ref[...]], axis=0)
        o_ref[pl.ds(qoff, qn), :] = _pv_heads(p_ref, qn, [(slice(0, kn + lc), vcat)]).astype(BF16)

    def step(tt, carry):
        pending = []
        for i in range(BLOCKS_PER_STEP):
            pending.append((scores(tt * BLOCKS_PER_STEP + i), p_refs[i]))
            if len(pending) > SCORES_AHEAD:
                finish(*pending.pop(0))
        while pending:
            finish(*pending.pop(0))
        return carry

    lax.fori_loop(0, nblk // BLOCKS_PER_STEP, step, 0)

    if need_ctx:
        s = lax.dot_general(_stack_heads(qc_ref[...]), kc_ref[...], _NT, preferred_element_type=F32)
        _softmax_rows([s], p_refs[0])
        oc_ref[...] = _pv_heads(p_refs[0], lc, [(slice(0, lc), vc_ref[...])]).astype(BF16)


def _rpb_rows(rpb):
    h, n_dr, _ = rpb.shape
    gap = jnp.zeros((h, n_dr, LANES - (2 * WIN_W - 1)), F32)
    return jnp.concatenate([rpb[..., WIN_W - 1:].astype(F32), gap, rpb[..., :WIN_W - 1].astype(F32)], axis=-1)


def _attention(qkv_x, qkv_c, rp, batch, need_ctx):
    tx, d3 = qkv_x.shape
    d = d3 // 3
    n = tx // batch
    l = qkv_c.shape[0] // batch
    rows = n // GRID_W
    kh = min(WIN_H, rows)
    npair = d // LANES
    band, plans = _band_plan(rows, kh)
    n_dr = rp.shape[1]
    kern = functools.partial(_attn_kernel, rows=rows, kh=kh, need_ctx=need_ctx)
    in_specs = [
        pl.BlockSpec((n, LANES), lambda p, b: (b, p)),
        pl.BlockSpec((n, LANES), lambda p, b: (b, npair + p)),
        pl.BlockSpec((n, LANES), lambda p, b: (b, 2 * npair + p)),
        pl.BlockSpec((l, LANES), lambda p, b: (b, p)),
        pl.BlockSpec((l, LANES), lambda p, b: (b, npair + p)),
        pl.BlockSpec((l, LANES), lambda p, b: (b, 2 * npair + p)),
        pl.BlockSpec((HEADS_PER_BLOCK, n_dr, LANES), lambda p, b: (p, 0, 0)),
    ]
    out_shape = [jax.ShapeDtypeStruct((tx, d), BF16)]
    out_specs = [pl.BlockSpec((n, LANES), lambda p, b: (b, p))]
    if need_ctx:
        out_shape.append(jax.ShapeDtypeStruct((batch * l, d), BF16))
        out_specs.append(pl.BlockSpec((l, LANES), lambda p, b: (b, p)))
    qrows = HEADS_PER_BLOCK * QROWS * GRID_W
    assert (rows // QROWS) % BLOCKS_PER_STEP == 0 and HEADS_PER_BLOCK * l <= qrows
    bias_bytes = len(plans) * qrows * band * GRID_W * 4
    est = (2 * (4 * n * LANES * 2 + 4 * l * LANES * 2) + bias_bytes
           + HEADS_PER_BLOCK * n_dr * GRID_W * LANES * 4
           + BLOCKS_PER_STEP * qrows * (band * GRID_W + l) * (2 + 4))
    outs = pl.pallas_call(
        kern,
        grid=(npair, batch),
        in_specs=in_specs,
        out_specs=out_specs,
        out_shape=out_shape,
        scratch_shapes=[pltpu.VMEM((HEADS_PER_BLOCK, n_dr, GRID_W, LANES), F32),
                        pltpu.VMEM((len(plans), qrows, band * GRID_W), F32)]
        + [pltpu.VMEM((qrows, band * GRID_W + l), BF16)] * BLOCKS_PER_STEP,
        compiler_params=_params(est, 2),
        name="nbr_attention",
    )(qkv_x, qkv_x, qkv_x, qkv_c, qkv_c, qkv_c, rp)
    return (outs[0], outs[1]) if need_ctx else (outs[0], None)


def _proj_ffn_kernel(x_ref, a_ref, mod_ref, wo_ref, g_ref, w1_ref, w3_ref, w2_ref, *rest,
                     tm, tpg, gbase, fc, final):
    final_g_ref = rest[0] if final else None
    o_ref, hid_ref = rest[-2:]
    d = x_ref.shape[-1]
    m = _mod_row(mod_ref, tm, tpg, gbase)
    y = jnp.dot(a_ref[...], wo_ref[...], preferred_element_type=F32)
    x1 = x_ref[...] + _chunk(m, 2, d) * y
    _finish(_ffn_tail(x1, m, g_ref[...], w1_ref, w3_ref, w2_ref, hid_ref, fc), final_g_ref, o_ref)


def _conv_ffn_kernel(x_ref, z_ref, zp_ref, zn_ref, gb_ref, cw_ref, mod_ref, wo_ref, g_ref,
                     w1_ref, w3_ref, w2_ref, o_ref, hid_ref, *, tm, tpg, gbase, fc, seq):
    d = x_ref.shape[-1]
    i = pl.program_id(0)
    m = _mod_row(mod_ref, tm, tpg, gbase)
    pos0 = lax.rem(i * tm, seq)
    z = z_ref[...].astype(F32)
    prev = zp_ref[...].astype(F32)[BF16_SUBLANES - 1:, :]
    nxt = zn_ref[...].astype(F32)[:1, :]
    prev = jnp.where(pos0 == 0, jnp.zeros_like(prev), prev)
    nxt = jnp.where(pos0 + tm == seq, jnp.zeros_like(nxt), nxt)
    row = lax.broadcasted_iota(jnp.int32, z.shape, 0)
    z_m1 = jnp.where(row == 0, prev, pltpu.roll(z, 1, 0))
    z_p1 = jnp.where(row == tm - 1, nxt, pltpu.roll(z, tm - 1, 0))
    cw = cw_ref[...]
    conv = z_m1 * cw[0:1] + z * cw[1:2] + z_p1 * cw[2:3]
    a = (gb_ref[...].astype(F32) * conv).astype(BF16)
    y = jnp.dot(a, wo_ref[...], preferred_element_type=F32)
    x1 = x_ref[...] + _chunk(m, 2, d) * y
    _finish(_ffn_tail(x1, m, g_ref[...], w1_ref, w3_ref, w2_ref, hid_ref, fc), None, o_ref)


POOL_PAD = 2 * F32_SUBLANES
POOL_CHUNK_ROWS = 256


def _window_sum(hext_ref, buf_a, buf_b, cols, levels, r0, rc):
    lo, hi = F32_SUBLANES, rc + POOL_PAD + F32_SUBLANES

    def pair(load, rows0, n, shift_lo, shift_hi):
        return load(rows0 - shift_lo, n) + load(rows0 + shift_hi, n)

    src = lambda e0, n: hext_ref[pl.ds(r0 + e0, n), cols]
    shift_lo, shift_hi = 1, 0
    for level in range(1, levels):
        dst = buf_a if level % 2 == 1 else buf_b
        dst[pl.ds(lo, hi - lo), :] = pair(src, lo, hi - lo, shift_lo, shift_hi)
        src = lambda e0, n, ref=dst: ref[pl.ds(e0, n), :]
        shift_lo = shift_hi = 2 ** (level - 1)
    return pair(src, POOL_PAD, rc, shift_lo, shift_hi)


def _pool_ffn_kernel(x_ref, xp_ref, xn_ref, mod_ref, gm_ref, pw_ref, ps_ref, g_ref,
                     w1_ref, w3_ref, w2_ref, o_ref, hid_ref, hext_ref, buf_a, buf_b, x1_ref,
                     *, tm, tpg, gbase, fc, seq):
    d = x_ref.shape[-1]
    halo = F32_SUBLANES
    i = pl.program_id(0)
    m = _mod_row(mod_ref, tm, tpg, gbase)
    shift, scale = _chunk(m, 0, d), _chunk(m, 1, d)
    gm = gm_ref[...]
    pos0 = lax.rem(i * tm, seq)
    h_top = _norm_mod(xp_ref[...], gm, shift, scale)
    h_bot = _norm_mod(xn_ref[...], gm, shift, scale)
    hext_ref[0:halo, :] = jnp.zeros((halo, d), F32)
    hext_ref[halo:POOL_PAD, :] = jnp.where(pos0 == 0, jnp.zeros_like(h_top), h_top)
    hext_ref[POOL_PAD:POOL_PAD + tm, :] = _norm_mod(x_ref[...], gm, shift, scale)
    hext_ref[POOL_PAD + tm:POOL_PAD + tm + halo, :] = jnp.where(pos0 + tm == seq, jnp.zeros_like(h_bot), h_bot)
    hext_ref[POOL_PAD + tm + halo:, :] = jnp.zeros((halo, d), F32)
    rc = min(tm, POOL_CHUNK_ROWS)
    for buf in (buf_a, buf_b):
        buf[0:halo, :] = jnp.zeros((halo, buf.shape[1]), F32)
        buf[POOL_PAD + rc + halo:, :] = jnp.zeros((halo, buf.shape[1]), F32)
    gc = d // len(POOL_WINDOWS)

    def pool_group(r0, gi, win):
        cols = slice(gi * gc, (gi + 1) * gc)
        pos = pos0 + r0 + lax.broadcasted_iota(jnp.int32, (rc, 1), 0)
        acc = _window_sum(hext_ref, buf_a, buf_b, cols, win.bit_length() - 1, r0, rc)
        cnt = (jnp.minimum(pos + win // 2, seq) - jnp.maximum(pos - win // 2, 0)).astype(F32)
        pooled = (acc / cnt - hext_ref[pl.ds(POOL_PAD + r0, rc), cols]).astype(BF16)
        y = jnp.dot(pooled, pw_ref[gi], preferred_element_type=F32) * ps_ref[:, cols]
        x1_ref[pl.ds(r0, rc), cols] = x_ref[pl.ds(r0, rc), cols] + m[:, 2 * d + gi * gc:2 * d + (gi + 1) * gc] * y

    def chunk_jobs(r0):
        return [functools.partial(pool_group, r0, gi, win) for gi, win in enumerate(POOL_WINDOWS)]

    for job in chunk_jobs(0):
        job()
    for r0 in range(0, tm, rc):
        nxt = chunk_jobs(r0 + rc) if r0 + rc < tm else []
        out = _ffn_tail(x1_ref[pl.ds(r0, rc), :], m, g_ref[...], w1_ref, w3_ref, w2_ref,
                        hid_ref.at[pl.ds(r0, rc), :], fc, side_jobs=nxt)
        o_ref[pl.ds(r0, rc), :] = out


def _ffn_est(tm, d, f):
    return 3 * d * f * 2 + 4 * tm * d * 4 + tm * f * 2 + 6 * tm * d * 4 + 3 * tm * 512 * 4


def _ffn_consts(g_ffn, w1, w3, w2):
    return ([g_ffn, w1, w3, w2],
            [_const_spec(g_ffn.shape), _const_spec(w1.shape), _const_spec(w3.shape), _const_spec(w2.shape)])


def _proj_ffn(x, a, mod, wo, ffn, tm, tpg, gbase, fc, final_g=None):
    t, d = x.shape
    f = ffn[1].shape[-1]
    wo, wo_spec = _weight_arg(wo)
    tile = pl.BlockSpec((tm, d), lambda i: (i, 0))
    fargs, fspecs = _ffn_consts(*ffn)
    args = [x, a, mod, wo] + fargs
    specs = [tile, tile, _const_spec(mod.shape), wo_spec] + fspecs
    if final_g is not None:
        args.append(final_g)
        specs.append(_const_spec(final_g.shape))
    kern = functools.partial(_proj_ffn_kernel, tm=tm, tpg=tpg, gbase=gbase, fc=fc, final=final_g is not None)
    return pl.pallas_call(
        kern, grid=(t // tm,), in_specs=specs, out_specs=tile,
        out_shape=jax.ShapeDtypeStruct((t, d), F32),
        scratch_shapes=[pltpu.VMEM((tm, f), BF16)],
        compiler_params=_params(_ffn_est(tm, d, f) + d * d * 2 + 2 * tm * d * 2),
        name="proj_ffn",
    )(*args)


def _conv_ffn(x, z, gb, conv_w, mod, wo, ffn, tm, tpg, gbase, fc, seq):
    t, d = x.shape
    f = ffn[1].shape[-1]
    wo, wo_spec = _weight_arg(wo)
    tile = pl.BlockSpec((tm, d), lambda i: (i, 0))
    hb = tm // BF16_SUBLANES
    nhb = t // BF16_SUBLANES
    prev = pl.BlockSpec((BF16_SUBLANES, d), lambda i: (jnp.maximum(i * hb - 1, 0), 0))
    nxt = pl.BlockSpec((BF16_SUBLANES, d), lambda i: (jnp.minimum((i + 1) * hb, nhb - 1), 0))
    fargs, fspecs = _ffn_consts(*ffn)
    kern = functools.partial(_conv_ffn_kernel, tm=tm, tpg=tpg, gbase=gbase, fc=fc, seq=seq)
    return pl.pallas_call(
        kern, grid=(t // tm,),
        in_specs=[tile, tile, prev, nxt, tile, _const_spec(conv_w.shape), _const_spec(mod.shape),
                  wo_spec] + fspecs,
        out_specs=tile,
        out_shape=jax.ShapeDtypeStruct((t, d), F32),
        scratch_shapes=[pltpu.VMEM((tm, f), BF16)],
        compiler_params=_params(_ffn_est(tm, d, f) + d * d * 2 + 4 * tm * d * 2 + 4 * tm * d * 4),
        name="conv_ffn",
    )(x, z, z, z, gb, conv_w, mod, wo, *fargs)


def _pool_ffn(x, mod, g_mix, pool_w, pool_scale, ffn, tm, tpg, gbase, fc, seq):
    t, d = x.shape
    f = ffn[1].shape[-1]
    pool_w, pw_spec = _weight_arg(pool_w)
    assert all(w & (w - 1) == 0 and w // 2 <= F32_SUBLANES for w in POOL_WINDOWS)
    tile = pl.BlockSpec((tm, d), lambda i: (i, 0))
    hb = tm // F32_SUBLANES
    nhb = t // F32_SUBLANES
    prev = pl.BlockSpec((F32_SUBLANES, d), lambda i: (jnp.maximum(i * hb - 1, 0), 0))
    nxt = pl.BlockSpec((F32_SUBLANES, d), lambda i: (jnp.minimum((i + 1) * hb, nhb - 1), 0))
    fargs, fspecs = _ffn_consts(*ffn)
    kern = functools.partial(_pool_ffn_kernel, tm=tm, tpg=tpg, gbase=gbase, fc=fc, seq=seq)
    return pl.pallas_call(
        kern, grid=(t // tm,),
        in_specs=[tile, prev, nxt, _const_spec(mod.shape), _const_spec(g_mix.shape),
                  pw_spec, _const_spec(pool_scale.shape)] + fspecs,
        out_specs=tile,
        out_shape=jax.ShapeDtypeStruct((t, d), F32),
        scratch_shapes=[pltpu.VMEM((tm, f), BF16), pltpu.VMEM((tm + 2 * POOL_PAD, d), F32)]
        + [pltpu.VMEM((min(tm, POOL_CHUNK_ROWS) + 2 * POOL_PAD, d // len(POOL_WINDOWS)), F32)] * 2
        + [pltpu.VMEM((tm, d), F32)],
        compiler_params=_params(_ffn_est(tm, d, f) + 4 * tm * d * 4),
        name="pool_ffn",
    )(x, x, x, mod, g_mix, pool_w, pool_scale, *fargs)


def _token_tile(seq):
    return min(seq, 1024)


def kernel(x, c, ctx, c_ctx, adaln_w, adaln_b, norm_mix_g, norm_ffn_g, ffn_w1, ffn_w3, ffn_w2,
           na_w_qkv, na_w_o, na_rpb, sc_w_in, sc_conv_w, sc_w_out, pool_w, pool_scale, final_g):
    batch, n, d = x.shape
    l = ctx.shape[1]
    depth = adaln_w.shape[0]
    f = ffn_w1.shape[-1]
    assert batch + 1 <= MOD_ROWS and d % LANES == 0 and n % GRID_W == 0
    assert d // N_HEADS * HEADS_PER_BLOCK == LANES
    fc = 256
    assert f % fc == 0
    assert (depth - 1) % N_MIXERS == 0, "the final norm is fused into the attention-layer tail"
    tm_x, tm_c = _token_tile(n), _token_tile(l)
    q_scale = float(d // N_HEADS) ** -0.5 * LOG2_E

    c_rows = jnp.concatenate([c, c_ctx[None, :], jnp.zeros((MOD_ROWS - batch - 1, d), F32)], axis=0)
    mod_all = _modulation(c_rows, adaln_w, adaln_b)

    gcw = d // len(POOL_WINDOWS)
    pool_rows = pool_w.reshape(pool_w.shape[0], len(POOL_WINDOWS) * gcw, gcw)
    tail_stacks = {0: na_w_o, 1: sc_w_out, 2: pool_rows}

    def tail_weights(layer):
        return [(ffn_w1, layer), (ffn_w3, layer), (ffn_w2, layer),
                (tail_stacks[layer % N_MIXERS], layer // N_MIXERS)]

    steps_x = batch * n // tm_x
    side_ok = all(_cast_rows_per_step(st.shape[1], steps_x) is not None
                  for layer in range(depth) for st, _ in tail_weights(layer))
    cast_plan, tail_bf16 = {}, {}
    host = None
    for layer in range(depth):
        if layer % N_MIXERS != N_MIXERS - 1 and side_ok:
            host = layer
        if host is None:
            tail_bf16[layer] = tuple(st[li].astype(BF16) for st, li in tail_weights(layer))
        else:
            cast_plan.setdefault(host, []).append(layer)

    xs = x.reshape(batch * n, d)
    cs = ctx.reshape(batch * l, d)
    x_grp = dict(tm=tm_x, tpg=n, gbase=0)
    c_grp = dict(tm=tm_c, tpg=batch * l, gbase=batch)

    for layer in range(depth):
        kind = layer % N_MIXERS
        j = layer // N_MIXERS
        need_ctx = layer < depth - 1
        mod = mod_all[layer]
        g_mix = norm_mix_g[layer].reshape(1, d)
        g_ffn = norm_ffn_g[layer].reshape(1, d)
        fin = final_g.reshape(1, d) if layer == depth - 1 else None
        hosted = cast_plan.get(layer, [])
        casts = [w for li in hosted for w in tail_weights(li)]
        n_tail = len(tail_weights(layer))

        if kind == 0:
            wqkv = (na_w_qkv, j)
            rp = _rpb_rows(na_rpb[j])
            qkv_x, cast_outs = _qkv_proj(xs, mod, g_mix, wqkv, q_scale=q_scale, casts=casts, **x_grp)
            tail_bf16.update(_split_casts(hosted, cast_outs, n_tail))
            qkv_c, _ = _qkv_proj(cs, mod, g_mix, wqkv, q_scale=q_scale, **c_grp)
            ffn = (g_ffn,) + tail_bf16[layer][:3]
            wo = tail_bf16[layer][3]
            o_x, o_c = _attention(qkv_x, qkv_c, rp, batch, need_ctx)
            xs = _proj_ffn(xs, o_x, mod, wo, ffn, fc=fc, final_g=fin, **x_grp)
            if need_ctx:
                cs = _proj_ffn(cs, o_c, mod, wo, ffn, fc=fc, **c_grp)
        elif kind == 1:
            w_in = (sc_w_in, j)
            gb_x, z_x, cast_outs = _conv_in_proj(xs, mod, g_mix, w_in, casts=casts, **x_grp)
            tail_bf16.update(_split_casts(hosted, cast_outs, n_tail))
            ffn = (g_ffn,) + tail_bf16[layer][:3]
            wo = tail_bf16[layer][3]
            xs_new = _conv_ffn(xs, z_x, gb_x, sc_conv_w[j], mod, wo, ffn, fc=fc, seq=n, **x_grp)
            if need_ctx:
                gb_c, z_c, _ = _conv_in_proj(cs, mod, g_mix, w_in, **c_grp)
                cs = _conv_ffn(cs, z_c, gb_c, sc_conv_w[j], mod, wo, ffn, fc=fc, seq=l, **c_grp)
            xs = xs_new
        else:
            pw = tail_bf16[layer][3].reshape(len(POOL_WINDOWS), gcw, gcw)
            ps = pool_scale[j].reshape(1, d)
            ffn = (g_ffn,) + tail_bf16[layer][:3]
            xs_new = _pool_ffn(xs, mod, g_mix, pw, ps, ffn, fc=fc, seq=n, **x_grp)
            if need_ctx:
                cs = _pool_ffn(cs, mod, g_mix, pw, ps, ffn, fc=fc, seq=l, **c_grp)
            xs = xs_new
    return xs.reshape(batch, n, d)
```

```python
import functools

import numpy as np
import jax
import jax.numpy as jnp
from jax import lax
from jax.experimental import pallas as pl
from jax.experimental.pallas import tpu as pltpu

F32 = jnp.float32
BF16 = jnp.bfloat16

GRID_W = 64
N_MIXERS = 3
N_HEADS = 16
WIN_H = 8
WIN_W = 16
CONV_W = 3
POOL_WINDOWS = (2, 4, 8, 16)
N_MOD = 6
EPS = 1e-6
LOG2_E = 1.4426950408889634

V7X_VMEM_BYTES = 64 * 1024 * 1024
LANES = 128
F32_SUBLANES = 8
BF16_SUBLANES = 16

HEADS_PER_BLOCK = 2
MOD_ROWS = 8


def _vmem_limit(est_bytes):
    return int(min(V7X_VMEM_BYTES - 6 * 1024 * 1024, max(32 * 1024 * 1024, est_bytes * 5 // 4)))


def _params(est_bytes, ndims=1):
    return pltpu.CompilerParams(dimension_semantics=("arbitrary",) * ndims,
                                vmem_limit_bytes=_vmem_limit(est_bytes))


def _const_spec(shape):
    nd = len(shape)
    return pl.BlockSpec(shape, lambda *_: (0,) * nd, pipeline_mode=pl.Buffered(1))


def _weight_arg(w):
    if isinstance(w, tuple):
        return w[0], _layer_spec(*w)
    return w, _const_spec(w.shape)


def _layer_spec(stack, idx):
    nd = stack.ndim
    return pl.BlockSpec((None,) + stack.shape[1:], lambda *_: (idx,) + (0,) * (nd - 1),
                        pipeline_mode=pl.Buffered(1))


def _split_bf16(v):
    hi = v.astype(BF16)
    lo = (v - hi.astype(F32)).astype(BF16)
    return hi, lo


def _mod_kernel(c_ref, w_ref, b_ref, o_ref):
    c = c_ref[...]
    s = c * jax.nn.sigmoid(c)
    s_hi, s_lo = _split_bf16(s)
    lhs = jnp.concatenate([s_hi, s_lo], axis=0)
    w_hi, w_lo = _split_bf16(w_ref[...])
    r = (jnp.dot(lhs, w_hi, preferred_element_type=F32)
         + jnp.dot(lhs, w_lo, preferred_element_type=F32))
    o_ref[...] = r[:MOD_ROWS] + r[MOD_ROWS:] + b_ref[...]


def _modulation(c_rows, adaln_w, adaln_b):
    depth, d, nm = adaln_w.shape
    nc = nm // 4
    est = 2 * d * nc * 4 + 3 * d * nc * 2 + 4 * MOD_ROWS * nc * 4
    return pl.pallas_call(
        _mod_kernel,
        grid=(depth, nm // nc),
        in_specs=[
            pl.BlockSpec((MOD_ROWS, d), lambda l, j: (0, 0)),
            pl.BlockSpec((None, d, nc), lambda l, j: (l, 0, j)),
            pl.BlockSpec((None, 1, nc), lambda l, j: (l, 0, j)),
        ],
        out_specs=pl.BlockSpec((None, MOD_ROWS, nc), lambda l, j: (l, 0, j)),
        out_shape=jax.ShapeDtypeStruct((depth, MOD_ROWS, nm), F32),
        compiler_params=_params(est, 2),
        name="adaln_mod",
    )(c_rows, adaln_w, adaln_b.reshape(depth, 1, nm))


def _mod_row(mod_ref, tm, tokens_per_group, group_base):
    grp = group_base + lax.div(pl.program_id(0) * tm, tokens_per_group)
    return mod_ref[pl.ds(grp, 1), :]


def _chunk(m, idx, d):
    return m[:, idx * d:(idx + 1) * d]


def _rms(x, g):
    ms = jnp.mean(x * x, axis=-1, keepdims=True)
    return x * lax.rsqrt(ms + EPS) * g


def _norm_mod(x, g, shift, scale):
    return _rms(x, g) * (1.0 + scale) + shift


def _ffn_tail(x1, m, g_ffn, w1_ref, w3_ref, w2_ref, hid_ref, fc, side_jobs=()):
    d = x1.shape[-1]
    f = w1_ref.shape[-1]
    side_jobs = list(side_jobs)
    h = _norm_mod(x1, g_ffn, _chunk(m, 3, d), _chunk(m, 4, d)).astype(BF16)
    for c in range(f // fc):
        sl = slice(c * fc, (c + 1) * fc)
        a = jnp.dot(h, w1_ref[:, sl], preferred_element_type=F32)
        b = jnp.dot(h, w3_ref[:, sl], preferred_element_type=F32)
        hid_ref[:, sl] = (a * jax.nn.sigmoid(a) * b).astype(BF16)
        if side_jobs:
            side_jobs.pop(0)()
    while side_jobs:
        side_jobs.pop(0)()
    y = jnp.dot(hid_ref[...], w2_ref[...], preferred_element_type=F32)
    return x1 + _chunk(m, 5, d) * y


def _finish(out, final_g_ref, o_ref):
    if final_g_ref is not None:
        out = _rms(out, final_g_ref[...])
    o_ref[...] = out


def _cast_side_job(w_ref, refs, n_cast, n_out):
    wbf_ref = refs[-1]

    @pl.when(pl.program_id(0) == 0)
    def _():
        wbf_ref[...] = w_ref[...].astype(BF16)

    for src, dst in zip(refs[:n_cast], refs[n_cast + n_out:-1]):
        dst[...] = src[...].astype(BF16)
    return refs[n_cast:n_cast + n_out], wbf_ref


def _qkv_kernel(x_ref, mod_ref, g_ref, w_ref, *refs, tm, tpg, gbase, q_scale, n_cast):
    (o_ref,), w_ref = _cast_side_job(w_ref, refs, n_cast, 1)
    d = x_ref.shape[-1]
    m = _mod_row(mod_ref, tm, tpg, gbase)
    h = _norm_mod(x_ref[...], g_ref[...], _chunk(m, 0, d), _chunk(m, 1, d)).astype(BF16)
    for j in range(3):
        r = jnp.dot(h, w_ref[:, j * d:(j + 1) * d], preferred_element_type=F32)
        if j == 0:
            r = r * q_scale
        o_ref[:, j * d:(j + 1) * d] = r.astype(BF16)


def _conv_in_kernel(x_ref, mod_ref, g_ref, w_ref, *refs, tm, tpg, gbase, n_cast):
    (gb_ref, z_ref), w_ref = _cast_side_job(w_ref, refs, n_cast, 2)
    d = x_ref.shape[-1]
    m = _mod_row(mod_ref, tm, tpg, gbase)
    h = _norm_mod(x_ref[...], g_ref[...], _chunk(m, 0, d), _chunk(m, 1, d)).astype(BF16)
    gb_ref[...] = jnp.dot(h, w_ref[:, :d], preferred_element_type=F32).astype(BF16)
    gc = jnp.dot(h, w_ref[:, d:2 * d], preferred_element_type=F32)
    val = jnp.dot(h, w_ref[:, 2 * d:], preferred_element_type=F32)
    z_ref[...] = (gc * val).astype(BF16)


def _cast_rows_per_step(n_rows, n_steps):
    if n_rows % n_steps == 0 and (n_rows // n_steps) % BF16_SUBLANES == 0:
        return n_rows // n_steps
    return None


def _front_call(kernel, x, mod, g, w, out_shapes, out_specs, tm, name, casts):
    t, d = x.shape
    w, wi = w
    nout = w.shape[-1]
    steps = t // tm
    est = 2 * tm * d * 4 + d * nout * (4 + 2) + 2 * tm * nout * 2 + tm * nout * 4 + 4 * tm * d * 4
    cast_args, cast_in_specs, cast_out_specs, cast_shapes = [], [], [], []
    for stack, li in casts:
        _, r, c = stack.shape
        rb = _cast_rows_per_step(r, steps)
        cast_args.append(stack)
        cast_in_specs.append(pl.BlockSpec((None, rb, c), lambda i, li=li: (li, i, 0)))
        cast_out_specs.append(pl.BlockSpec((rb, c), lambda i: (i, 0)))
        cast_shapes.append(jax.ShapeDtypeStruct((r, c), BF16))
        est += 2 * rb * c * (4 + 2)
    outs = pl.pallas_call(
        functools.partial(kernel, n_cast=len(casts)),
        grid=(steps,),
        in_specs=[
            pl.BlockSpec((tm, d), lambda i: (i, 0)),
            _const_spec(mod.shape),
            _const_spec(g.shape),
            _layer_spec(w, wi),
        ] + cast_in_specs,
        out_specs=list(out_specs) + cast_out_specs,
        out_shape=list(out_shapes) + cast_shapes,
        scratch_shapes=[pltpu.VMEM((d, nout), BF16)],
        compiler_params=_params(est),
        name=name,
    )(x, mod, g, w, *cast_args)
    n_own = len(out_shapes)
    return outs[:n_own], outs[n_own:]


def _qkv_proj(x, mod, g, w, tm, tpg, gbase, q_scale, casts=()):
    t, d = x.shape
    kern = functools.partial(_qkv_kernel, tm=tm, tpg=tpg, gbase=gbase, q_scale=q_scale)
    (qkv,), cast = _front_call(kern, x, mod, g, w, [jax.ShapeDtypeStruct((t, 3 * d), BF16)],
                               [pl.BlockSpec((tm, 3 * d), lambda i: (i, 0))], tm, "qkv_proj", casts)
    return qkv, cast


def _conv_in_proj(x, mod, g, w, tm, tpg, gbase, casts=()):
    t, d = x.shape
    kern = functools.partial(_conv_in_kernel, tm=tm, tpg=tpg, gbase=gbase)
    spec = pl.BlockSpec((tm, d), lambda i: (i, 0))
    (gb, z), cast = _front_call(kern, x, mod, g, w, [jax.ShapeDtypeStruct((t, d), BF16)] * 2, [spec, spec],
                                tm, "conv_in_proj", casts)
    return gb, z, cast


def _split_casts(layers, cast_outs, per_layer):
    return {li: tuple(cast_outs[per_layer * k:per_layer * (k + 1)]) for k, li in enumerate(layers)}


def _stack_heads(q):
    lane = lax.broadcasted_iota(jnp.int32, q.shape, 1)
    zero = jnp.zeros_like(q)
    return jnp.concatenate([jnp.where(lane < LANES // 2, q, zero),
                            jnp.where(lane >= LANES // 2, q, zero)], axis=0)


_NT = (((1,), (1,)), ((), ()))


QROWS = 4


def _band_plan(rows, kh):
    band = QROWS + kh
    assert rows % QROWS == 0 and band % 2 == 0 and rows >= band + QROWS

    def plan(r0):
        ub = int(np.clip(r0 - kh // 2, 0, rows - band))
        tab = np.full((QROWS, band), -1, np.int64)
        for ri in range(QROWS):
            r = r0 + ri
            start = int(np.clip(r - kh // 2, 0, rows - kh))
            for u in range(band):
                if start <= ub + u < start + kh:
                    tab[ri, u] = ub + u - r + WIN_H - 1
        return ub - r0, tab

    first, last = plan(0), plan(rows - QROWS)
    interior = plan(QROWS)
    for r0 in range(QROWS, rows - QROWS, QROWS):
        off, tab = plan(r0)
        assert off == interior[0] and (tab == interior[1]).all()
    return band, [first, interior, last]


def _build_bias(rp_ref, xm_ref, bias_ref, plans):
    half = LANES // 2
    shape = (GRID_W, LANES)
    qcol = lax.broadcasted_iota(jnp.int32, shape, 0)
    lane = lax.broadcasted_iota(jnp.int32, shape, 1)
    kcol = jnp.bitwise_and(lane, half - 1)
    col_start = jnp.clip(qcol - WIN_W // 2, 0, GRID_W - WIN_W)
    in_window = (kcol >= col_start) & (kcol < col_start + WIN_W)
    low = lane < half
    neg = jnp.full(shape, -jnp.inf, F32)
    n_dr = rp_ref.shape[1]
    for hh in range(HEADS_PER_BLOCK):
        for dr in range(n_dr):
            base = jnp.broadcast_to(rp_ref[hh, dr:dr + 1, :] * LOG2_E, shape)
            t_lo = pltpu.roll(base, 0, 1, stride=1, stride_axis=0)
            both = jnp.where(low, t_lo, pltpu.roll(t_lo, half, 1))
            xm_ref[hh, dr] = jnp.where(in_window, both, neg)
    for v, (_, tab) in enumerate(plans):
        for hh in range(HEADS_PER_BLOCK):
            for ri in range(QROWS):
                r_lo = (hh * QROWS + ri) * GRID_W
                for j in range(tab.shape[1] // 2):
                    da, db = int(tab[ri, 2 * j]), int(tab[ri, 2 * j + 1])
                    ta = xm_ref[hh, da] if da >= 0 else neg
                    tb = xm_ref[hh, db] if db >= 0 else neg
                    bias_ref[v, r_lo:r_lo + GRID_W, j * LANES:(j + 1) * LANES] = jnp.where(low, ta, tb)


SOFTMAX_SLAB = 8
BLOCKS_PER_STEP = 16
SCORES_AHEAD = BLOCKS_PER_STEP


def _softmax_rows(s_parts, p_ref):
    n_rows = s_parts[0].shape[0]
    for r in range(0, n_rows, SOFTMAX_SLAB):
        rs = slice(r, r + SOFTMAX_SLAB)
        blocks = [s[rs, c:c + LANES] for s in s_parts for c in range(0, s.shape[1], LANES)]
        mx = jnp.max(functools.reduce(jnp.maximum, blocks), axis=-1, keepdims=True)
        for j, blk in enumerate(blocks):
            p_ref[rs, j * LANES:(j + 1) * LANES] = jnp.exp2(blk - mx).astype(BF16)


def _pv_heads(p_ref, n_q, pieces):
    half = LANES // 2
    outs = []
    for hh in range(HEADS_PER_BLOCK):
        hs = slice(hh * n_q, (hh + 1) * n_q)
        o = None
        for cols, v in pieces:
            own = (lax.broadcasted_iota(jnp.int32, v.shape, 1) < half) == (hh == 0)
            part = jnp.dot(p_ref[hs, cols], jnp.where(own, v, jnp.ones_like(v)), preferred_element_type=F32)
            o = part if o is None else o + part
        outs.append(o)
    low = lax.broadcasted_iota(jnp.int32, (n_q, LANES), 1) < half
    num = jnp.where(low, outs[0], outs[1])
    den = pltpu.roll(jnp.where(low, outs[1], outs[0]), half, 1)
    return num / den


def _attn_kernel(q_ref, k_ref, v_ref, qc_ref, kc_ref, vc_ref, rp_ref, *rest, rows, kh, need_ctx):
    o_ref = rest[0]
    oc_ref = rest[1] if need_ctx else None
    xm_ref, bias_ref = rest[-2 - BLOCKS_PER_STEP:-BLOCKS_PER_STEP]
    p_refs = rest[-BLOCKS_PER_STEP:]
    band, plans = _band_plan(rows, kh)
    nblk = rows // QROWS
    qn = QROWS * GRID_W
    kn = band * GRID_W
    lc = kc_ref.shape[0]

    @pl.when(pl.program_id(1) == 0)
    def _():
        _build_bias(rp_ref, xm_ref, bias_ref, plans)

    def scores(t):
        r0 = t * QROWS
        ub = jnp.clip(r0 - kh // 2, 0, rows - band)
        variant = jnp.where(t > 0, 1, 0) + jnp.where(t == nblk - 1, 1, 0)
        qoff = pl.multiple_of(r0 * GRID_W, qn)
        koff = pl.multiple_of(ub * GRID_W, GRID_W)
        qm = _stack_heads(q_ref[pl.ds(qoff, qn), :])
        kcat = jnp.concatenate([k_ref[pl.ds(koff, kn), :], kc_ref[...]], axis=0)
        s = lax.dot_general(qm, kcat, _NT, preferred_element_type=F32)
        return qoff, koff, s[:, :kn] + bias_ref[variant], s[:, kn:]

    def finish(blk, p_ref):
        qoff, koff, s_lat, s_ctx = blk
        _softmax_rows([s_lat, s_ctx], p_ref)
        vcat = jnp.concatenate([v_ref[pl.ds(koff, kn), :], vc_ref[...]], axis=0)
        o_ref[pl.ds(qoff, qn), :] = _pv_heads(p_ref, qn, [(slice(0, kn + lc), vcat)]).astype(BF16)

    def step(tt, carry):
        pending = []
        for i in range(BLOCKS_PER_STEP):
            pending.append((scores(tt * BLOCKS_PER_STEP + i), p_refs[i]))
            if len(pending) > SCORES_AHEAD:
                finish(*pending.pop(0))
        while pending:
            finish(*pending.pop(0))
        return carry

    lax.fori_loop(0, nblk // BLOCKS_PER_STEP, step, 0)

    if need_ctx:
        s = lax.dot_general(_stack_heads(qc_ref[...]), kc_ref[...], _NT, preferred_element_type=F32)
        _softmax_rows([s], p_refs[0])
        oc_ref[...] = _pv_heads(p_refs[0], lc, [(slice(0, lc), vc_ref[...])]).astype(BF16)


def _rpb_rows(rpb):
    h, n_dr, _ = rpb.shape
    gap = jnp.zeros((h, n_dr, LANES - (2 * WIN_W - 1)), F32)
    return jnp.concatenate([rpb[..., WIN_W - 1:].astype(F32), gap, rpb[..., :WIN_W - 1].astype(F32)], axis=-1)


def _attention(qkv_x, qkv_c, rp, batch, need_ctx):
    tx, d3 = qkv_x.shape
    d = d3 // 3
    n = tx // batch
    l = qkv_c.shape[0] // batch
    rows = n // GRID_W
    kh = min(WIN_H, rows)
    npair = d // LANES
    band, plans = _band_plan(rows, kh)
    n_dr = rp.shape[1]
    kern = functools.partial(_attn_kernel, rows=rows, kh=kh, need_ctx=need_ctx)
    in_specs = [
        pl.BlockSpec((n, LANES), lambda p, b: (b, p)),
        pl.BlockSpec((n, LANES), lambda p, b: (b, npair + p)),
        pl.BlockSpec((n, LANES), lambda p, b: (b, 2 * npair + p)),
        pl.BlockSpec((l, LANES), lambda p, b: (b, p)),
        pl.BlockSpec((l, LANES), lambda p, b: (b, npair + p)),
        pl.BlockSpec((l, LANES), lambda p, b: (b, 2 * npair + p)),
        pl.BlockSpec((HEADS_PER_BLOCK, n_dr, LANES), lambda p, b: (p, 0, 0)),
    ]
    out_shape = [jax.ShapeDtypeStruct((tx, d), BF16)]
    out_specs = [pl.BlockSpec((n, LANES), lambda p, b: (b, p))]
    if need_ctx:
        out_shape.append(jax.ShapeDtypeStruct((batch * l, d), BF16))
        out_specs.append(pl.BlockSpec((l, LANES), lambda p, b: (b, p)))
    qrows = HEADS_PER_BLOCK * QROWS * GRID_W
    assert (rows // QROWS) % BLOCKS_PER_STEP == 0 and HEADS_PER_BLOCK * l <= qrows
    bias_bytes = len(plans) * qrows * band * GRID_W * 4
    est = (2 * (4 * n * LANES * 2 + 4 * l * LANES * 2) + bias_bytes
           + HEADS_PER_BLOCK * n_dr * GRID_W * LANES * 4
           + BLOCKS_PER_STEP * qrows * (band * GRID_W + l) * (2 + 4))
    outs = pl.pallas_call(
        kern,
        grid=(npair, batch),
        in_specs=in_specs,
        out_specs=out_specs,
        out_shape=out_shape,
        scratch_shapes=[pltpu.VMEM((HEADS_PER_BLOCK, n_dr, GRID_W, LANES), F32),
                        pltpu.VMEM((len(plans), qrows, band * GRID_W), F32)]
        + [pltpu.VMEM((qrows, band * GRID_W + l), BF16)] * BLOCKS_PER_STEP,
        compiler_params=_params(est, 2),
        name="nbr_attention",
    )(qkv_x, qkv_x, qkv_x, qkv_c, qkv_c, qkv_c, rp)
    return (outs[0], outs[1]) if need_ctx else (outs[0], None)


def _proj_ffn_kernel(x_ref, a_ref, mod_ref, wo_ref, g_ref, w1_ref, w3_ref, w2_ref, *rest,
                     tm, tpg, gbase, fc, final):
    final_g_ref = rest[0] if final else None
    o_ref, hid_ref = rest[-2:]
    d = x_ref.shape[-1]
    m = _mod_row(mod_ref, tm, tpg, gbase)
    y = jnp.dot(a_ref[...], wo_ref[...], preferred_element_type=F32)
    x1 = x_ref[...] + _chunk(m, 2, d) * y
    _finish(_ffn_tail(x1, m, g_ref[...], w1_ref, w3_ref, w2_ref, hid_ref, fc), final_g_ref, o_ref)


CONV_CHUNK_ROWS = 256


def _conv_ffn_kernel(x_ref, z_ref, zp_ref, zn_ref, gb_ref, cw_ref, mod_ref, wo_ref, g_ref,
                     w1_ref, w3_ref, w2_ref, o_ref, hid_ref, x1_ref, *, tm, tpg, gbase, fc, seq):
    d = x_ref.shape[-1]
    i = pl.program_id(0)
    m = _mod_row(mod_ref, tm, tpg, gbase)
    pos0 = lax.rem(i * tm, seq)
    tile_prev = zp_ref[...].astype(F32)[BF16_SUBLANES - 1:, :]
    tile_next = zn_ref[...].astype(F32)[:1, :]
    tile_prev = jnp.where(pos0 == 0, jnp.zeros_like(tile_prev), tile_prev)
    tile_next = jnp.where(pos0 + tm == seq, jnp.zeros_like(tile_next), tile_next)
    cw = cw_ref[...]
    rc = min(tm, CONV_CHUNK_ROWS)

    def mixer_chunk(r0):
        rows = pl.ds(r0, rc)
        z = z_ref[rows, :].astype(F32)
        prev = (tile_prev if r0 == 0 else
                z_ref[pl.ds(r0 - BF16_SUBLANES, BF16_SUBLANES), :].astype(F32)[BF16_SUBLANES - 1:, :])
        nxt = (tile_next if r0 + rc == tm else
               z_ref[pl.ds(r0 + rc, BF16_SUBLANES), :].astype(F32)[:1, :])
        row = lax.broadcasted_iota(jnp.int32, z.shape, 0)
        z_m1 = jnp.where(row == 0, prev, pltpu.roll(z, 1, 0))
        z_p1 = jnp.where(row == rc - 1, nxt, pltpu.roll(z, rc - 1, 0))
        conv = z_m1 * cw[0:1] + z * cw[1:2] + z_p1 * cw[2:3]
        a = (gb_ref[rows, :].astype(F32) * conv).astype(BF16)
        y = jnp.dot(a, wo_ref[...], preferred_element_type=F32)
        x1_ref[rows, :] = x_ref[rows, :] + _chunk(m, 2, d) * y

    mixer_chunk(0)
    for r0 in range(0, tm, rc):
        nxt_jobs = [functools.partial(mixer_chunk, r0 + rc)] if r0 + rc < tm else []
        out = _ffn_tail(x1_ref[pl.ds(r0, rc), :], m, g_ref[...], w1_ref, w3_ref, w2_ref,
                        hid_ref.at[pl.ds(r0, rc), :], fc, side_jobs=nxt_jobs)
        o_ref[pl.ds(r0, rc), :] = out


POOL_PAD = 2 * F32_SUBLANES
POOL_CHUNK_ROWS = 256


def _window_sum(hext_ref, buf_a, buf_b, cols, levels, r0, rc):
    lo, hi = F32_SUBLANES, rc + POOL_PAD + F32_SUBLANES

    def pair(load, rows0, n, shift_lo, shift_hi):
        return load(rows0 - shift_lo, n) + load(rows0 + shift_hi, n)

    src = lambda e0, n: hext_ref[pl.ds(r0 + e0, n), cols]
    shift_lo, shift_hi = 1, 0
    for level in range(1, levels):
        dst = buf_a if level % 2 == 1 else buf_b
        dst[pl.ds(lo, hi - lo), :] = pair(src, lo, hi - lo, shift_lo, shift_hi)
        src = lambda e0, n, ref=dst: ref[pl.ds(e0, n), :]
        shift_lo = shift_hi = 2 ** (level - 1)
    return pair(src, POOL_PAD, rc, shift_lo, shift_hi)


def _pool_ffn_kernel(x_ref, xp_ref, xn_ref, mod_ref, gm_ref, pw_ref, ps_ref, g_ref,
                     w1_ref, w3_ref, w2_ref, o_ref, hid_ref, hext_ref, buf_a, buf_b, x1_ref,
                     *, tm, tpg, gbase, fc, seq):
    d = x_ref.shape[-1]
    halo = F32_SUBLANES
    i = pl.program_id(0)
    m = _mod_row(mod_ref, tm, tpg, gbase)
    shift, scale = _chunk(m, 0, d), _chunk(m, 1, d)
    gm = gm_ref[...]
    pos0 = lax.rem(i * tm, seq)
    h_top = _norm_mod(xp_ref[...], gm, shift, scale)
    h_bot = _norm_mod(xn_ref[...], gm, shift, scale)
    hext_ref[0:halo, :] = jnp.zeros((halo, d), F32)
    hext_ref[halo:POOL_PAD, :] = jnp.where(pos0 == 0, jnp.zeros_like(h_top), h_top)
    hext_ref[POOL_PAD:POOL_PAD + tm, :] = _norm_mod(x_ref[...], gm, shift, scale)
    hext_ref[POOL_PAD + tm:POOL_PAD + tm + halo, :] = jnp.where(pos0 + tm == seq, jnp.zeros_like(h_bot), h_bot)
    hext_ref[POOL_PAD + tm + halo:, :] = jnp.zeros((halo, d), F32)
    rc = min(tm, POOL_CHUNK_ROWS)
    for buf in (buf_a, buf_b):
        buf[0:halo, :] = jnp.zeros((halo, buf.shape[1]), F32)
        buf[POOL_PAD + rc + halo:, :] = jnp.zeros((halo, buf.shape[1]), F32)
    gc = d // len(POOL_WINDOWS)

    def pool_group(r0, gi, win):
        cols = slice(gi * gc, (gi + 1) * gc)
        pos = pos0 + r0 + lax.broadcasted_iota(jnp.int32, (rc, 1), 0)
        acc = _window_sum(hext_ref, buf_a, buf_b, cols, win.bit_length() - 1, r0, rc)
        cnt = (jnp.minimum(pos + win // 2, seq) - jnp.maximum(pos - win // 2, 0)).astype(F32)
        pooled = (acc / cnt - hext_ref[pl.ds(POOL_PAD + r0, rc), cols]).astype(BF16)
        y = jnp.dot(pooled, pw_ref[gi], preferred_element_type=F32) * ps_ref[:, cols]
        x1_ref[pl.ds(r0, rc), cols] = x_ref[pl.ds(r0, rc), cols] + m[:, 2 * d + gi * gc:2 * d + (gi + 1) * gc] * y

    def chunk_jobs(r0):
        return [functools.partial(pool_group, r0, gi, win) for gi, win in enumerate(POOL_WINDOWS)]

    for job in chunk_jobs(0):
        job()
    for r0 in range(0, tm, rc):
        nxt = chunk_jobs(r0 + rc) if r0 + rc < tm else []
        out = _ffn_tail(x1_ref[pl.ds(r0, rc), :], m, g_ref[...], w1_ref, w3_ref, w2_ref,
                        hid_ref.at[pl.ds(r0, rc), :], fc, side_jobs=nxt)
        o_ref[pl.ds(r0, rc), :] = out


def _ffn_est(tm, d, f):
    return 3 * d * f * 2 + 4 * tm * d * 4 + tm * f * 2 + 6 * tm * d * 4 + 3 * tm * 512 * 4


def _ffn_consts(g_ffn, w1, w3, w2):
    return ([g_ffn, w1, w3, w2],
            [_const_spec(g_ffn.shape), _const_spec(w1.shape), _const_spec(w3.shape), _const_spec(w2.shape)])


def _proj_ffn(x, a, mod, wo, ffn, tm, tpg, gbase, fc, final_g=None):
    t, d = x.shape
    f = ffn[1].shape[-1]
    wo, wo_spec = _weight_arg(wo)
    tile = pl.BlockSpec((tm, d), lambda i: (i, 0))
    fargs, fspecs = _ffn_consts(*ffn)
    args = [x, a, mod, wo] + fargs
    specs = [tile, tile, _const_spec(mod.shape), wo_spec] + fspecs
    if final_g is not None:
        args.append(final_g)
        specs.append(_const_spec(final_g.shape))
    kern = functools.partial(_proj_ffn_kernel, tm=tm, tpg=tpg, gbase=gbase, fc=fc, final=final_g is not None)
    return pl.pallas_call(
        kern, grid=(t // tm,), in_specs=specs, out_specs=tile,
        out_shape=jax.ShapeDtypeStruct((t, d), F32),
        scratch_shapes=[pltpu.VMEM((tm, f), BF16)],
        compiler_params=_params(_ffn_est(tm, d, f) + d * d * 2 + 2 * tm * d * 2),
        name="proj_ffn",
    )(*args)


def _conv_ffn(x, z, gb, conv_w, mod, wo, ffn, tm, tpg, gbase, fc, seq):
    t, d = x.shape
    f = ffn[1].shape[-1]
    wo, wo_spec = _weight_arg(wo)
    tile = pl.BlockSpec((tm, d), lambda i: (i, 0))
    hb = tm // BF16_SUBLANES
    nhb = t // BF16_SUBLANES
    prev = pl.BlockSpec((BF16_SUBLANES, d), lambda i: (jnp.maximum(i * hb - 1, 0), 0))
    nxt = pl.BlockSpec((BF16_SUBLANES, d), lambda i: (jnp.minimum((i + 1) * hb, nhb - 1), 0))
    fargs, fspecs = _ffn_consts(*ffn)
    kern = functools.partial(_conv_ffn_kernel, tm=tm, tpg=tpg, gbase=gbase, fc=fc, seq=seq)
    return pl.pallas_call(
        kern, grid=(t // tm,),
        in_specs=[tile, tile, prev, nxt, tile, _const_spec(conv_w.shape), _const_spec(mod.shape),
                  wo_spec] + fspecs,
        out_specs=tile,
        out_shape=jax.ShapeDtypeStruct((t, d), F32),
        scratch_shapes=[pltpu.VMEM((tm, f), BF16), pltpu.VMEM((tm, d), F32)],
        compiler_params=_params(_ffn_est(tm, d, f) + d * d * 2 + 4 * tm * d * 2 + 4 * tm * d * 4),
        name="conv_ffn",
    )(x, z, z, z, gb, conv_w, mod, wo, *fargs)


def _pool_ffn(x, mod, g_mix, pool_w, pool_scale, ffn, tm, tpg, gbase, fc, seq):
    t, d = x.shape
    f = ffn[1].shape[-1]
    pool_w, pw_spec = _weight_arg(pool_w)
    assert all(w & (w - 1) == 0 and w // 2 <= F32_SUBLANES for w in POOL_WINDOWS)
    tile = pl.BlockSpec((tm, d), lambda i: (i, 0))
    hb = tm // F32_SUBLANES
    nhb = t // F32_SUBLANES
    prev = pl.BlockSpec((F32_SUBLANES, d), lambda i: (jnp.maximum(i * hb - 1, 0), 0))
    nxt = pl.BlockSpec((F32_SUBLANES, d), lambda i: (jnp.minimum((i + 1) * hb, nhb - 1), 0))
    fargs, fspecs = _ffn_consts(*ffn)
    kern = functools.partial(_pool_ffn_kernel, tm=tm, tpg=tpg, gbase=gbase, fc=fc, seq=seq)
    return pl.pallas_call(
        kern, grid=(t // tm,),
        in_specs=[tile, prev, nxt, _const_spec(mod.shape), _const_spec(g_mix.shape),
                  pw_spec, _const_spec(pool_scale.shape)] + fspecs,
        out_specs=tile,
        out_shape=jax.ShapeDtypeStruct((t, d), F32),
        scratch_shapes=[pltpu.VMEM((tm, f), BF16), pltpu.VMEM((tm + 2 * POOL_PAD, d), F32)]
        + [pltpu.VMEM((min(tm, POOL_CHUNK_ROWS) + 2 * POOL_PAD, d // len(POOL_WINDOWS)), F32)] * 2
        + [pltpu.VMEM((tm, d), F32)],
        compiler_params=_params(_ffn_est(tm, d, f) + 4 * tm * d * 4),
        name="pool_ffn",
    )(x, x, x, mod, g_mix, pool_w, pool_scale, *fargs)


def _token_tile(seq):
    return min(seq, 1024)


def kernel(x, c, ctx, c_ctx, adaln_w, adaln_b, norm_mix_g, norm_ffn_g, ffn_w1, ffn_w3, ffn_w2,
           na_w_qkv, na_w_o, na_rpb, sc_w_in, sc_conv_w, sc_w_out, pool_w, pool_scale, final_g):
    batch, n, d = x.shape
    l = ctx.shape[1]
    depth = adaln_w.shape[0]
    f = ffn_w1.shape[-1]
    assert batch + 1 <= MOD_ROWS and d % LANES == 0 and n % GRID_W == 0
    assert d // N_HEADS * HEADS_PER_BLOCK == LANES
    fc = 256
    assert f % fc == 0
    assert (depth - 1) % N_MIXERS == 0, "the final norm is fused into the attention-layer tail"
    tm_x, tm_c = _token_tile(n), _token_tile(l)
    q_scale = float(d // N_HEADS) ** -0.5 * LOG2_E

    c_rows = jnp.concatenate([c, c_ctx[None, :], jnp.zeros((MOD_ROWS - batch - 1, d), F32)], axis=0)
    mod_all = _modulation(c_rows, adaln_w, adaln_b)

    gcw = d // len(POOL_WINDOWS)
    pool_rows = pool_w.reshape(pool_w.shape[0], len(POOL_WINDOWS) * gcw, gcw)
    tail_stacks = {0: na_w_o, 1: sc_w_out, 2: pool_rows}

    def tail_weights(layer):
        return [(ffn_w1, layer), (ffn_w3, layer), (ffn_w2, layer),
                (tail_stacks[layer % N_MIXERS], layer // N_MIXERS)]

    steps_x = batch * n // tm_x
    side_ok = all(_cast_rows_per_step(st.shape[1], steps_x) is not None
                  for layer in range(depth) for st, _ in tail_weights(layer))
    cast_plan, tail_bf16 = {}, {}
    host = None
    for layer in range(depth):
        if layer % N_MIXERS != N_MIXERS - 1 and side_ok:
            host = layer
        if host is None:
            tail_bf16[layer] = tuple(st[li].astype(BF16) for st, li in tail_weights(layer))
        else:
            cast_plan.setdefault(host, []).append(layer)

    xs = x.reshape(batch * n, d)
    cs = ctx.reshape(batch * l, d)
    x_grp = dict(tm=tm_x, tpg=n, gbase=0)
    c_grp = dict(tm=tm_c, tpg=batch * l, gbase=batch)

    for layer in range(depth):
        kind = layer % N_MIXERS
        j = layer // N_MIXERS
        need_ctx = layer < depth - 1
        mod = mod_all[layer]
        g_mix = norm_mix_g[layer].reshape(1, d)
        g_ffn = norm_ffn_g[layer].reshape(1, d)
        fin = final_g.reshape(1, d) if layer == depth - 1 else None
        hosted = cast_plan.get(layer, [])
        casts = [w for li in hosted for w in tail_weights(li)]
        n_tail = len(tail_weights(layer))

        if kind == 0:
            wqkv = (na_w_qkv, j)
            rp = _rpb_rows(na_rpb[j])
            qkv_x, cast_outs = _qkv_proj(xs, mod, g_mix, wqkv, q_scale=q_scale, casts=casts, **x_grp)
            tail_bf16.update(_split_casts(hosted, cast_outs, n_tail))
            qkv_c, _ = _qkv_proj(cs, mod, g_mix, wqkv, q_scale=q_scale, **c_grp)
            ffn = (g_ffn,) + tail_bf16[layer][:3]
            wo = tail_bf16[layer][3]
            o_x, o_c = _attention(qkv_x, qkv_c, rp, batch, need_ctx)
            xs = _proj_ffn(xs, o_x, mod, wo, ffn, fc=fc, final_g=fin, **x_grp)
            if need_ctx:
                cs = _proj_ffn(cs, o_c, mod, wo, ffn, fc=fc, **c_grp)
        elif kind == 1:
            w_in = (sc_w_in, j)
            gb_x, z_x, cast_outs = _conv_in_proj(xs, mod, g_mix, w_in, casts=casts, **x_grp)
            tail_bf16.update(_split_casts(hosted, cast_outs, n_tail))
            ffn = (g_ffn,) + tail_bf16[layer][:3]
            wo = tail_bf16[layer][3]
            xs_new = _conv_ffn(xs, z_x, gb_x, sc_conv_w[j], mod, wo, ffn, fc=fc, seq=n, **x_grp)
            if need_ctx:
                gb_c, z_c, _ = _conv_in_proj(cs, mod, g_mix, w_in, **c_grp)
                cs = _conv_ffn(cs, z_c, gb_c, sc_conv_w[j], mod, wo, ffn, fc=fc, seq=l, **c_grp)
            xs = xs_new
        else:
            pw = tail_bf16[layer][3].reshape(len(POOL_WINDOWS), gcw, gcw)
            ps = pool_scale[j].reshape(1, d)
            ffn = (g_ffn,) + tail_bf16[layer][:3]
            xs_new = _pool_ffn(xs, mod, g_mix, pw, ps, ffn, fc=fc, seq=n, **x_grp)
            if need_ctx:
                cs = _pool_ffn(cs, mod, g_mix, pw, ps, ffn, fc=fc, seq=l, **c_grp)
            xs = xs_new
    return xs.reshape(batch, n, d)
```
